```python
import jax
import jax.numpy as jnp
from jax import lax
import numpy as np

D_MODEL = 1024
BATCH = 8
SEQ = 8192
DEPTH = 4

GRID_W = 64
CTX_LEN = 256
HEAD_DIM = 64
MIX_WIDTH = D_MODEL
POOL_WIDTH = MIX_WIDTH // 4
N_POOL = 4
POOL_DIM = POOL_WIDTH // N_POOL
POOL_WINDOWS = (2, 4, 8, 16)
NA_WIDTH = 3 * MIX_WIDTH // 8
NA_HEADS = NA_WIDTH // HEAD_DIM
NA_ROWS = 8
NA_COLS = 16
GQA_WIDTH = MIX_WIDTH - POOL_WIDTH - NA_WIDTH
GQA_Q_HEADS = GQA_WIDTH // HEAD_DIM
GQA_KV_HEADS = 2
GQA_KV_WIDTH = GQA_KV_HEADS * HEAD_DIM
IN_SIZES = (POOL_WIDTH, NA_WIDTH, NA_WIDTH, NA_WIDTH, GQA_WIDTH, GQA_KV_WIDTH, GQA_KV_WIDTH)
IN_WIDTH = POOL_WIDTH + 3 * NA_WIDTH + GQA_WIDTH + 2 * GQA_KV_WIDTH
Q_BLOCK = 128
ROPE_THETA = 10000.0
ROPE_AXIS_DIM = HEAD_DIM // 2
D_FF = 2816
CONV_WIDTH = 3
DEEPNORM_ALPHA = (2 * DEPTH) ** 0.25
DEEPNORM_BETA = (8 * DEPTH) ** -0.25
LN_EPS = 1e-6

kernel_name = 'hybrid_pool_natten_gqa_dit'


def layer_norm(x, g, b):
    xf = x.astype(jnp.float32)
    mu = jnp.mean(xf, -1, keepdims=True)
    var = jnp.mean(jnp.square(xf - mu), -1, keepdims=True)
    y = (xf - mu) * lax.rsqrt(var + LN_EPS) * g.astype(jnp.float32) + b.astype(jnp.float32)
    return y.astype(x.dtype)


def rms_norm(x, g):
    xf = x.astype(jnp.float32)
    y = xf * lax.rsqrt(jnp.mean(xf * xf, -1, keepdims=True) + LN_EPS) * g.astype(jnp.float32)
    return y.astype(x.dtype)


def axial_rope_tables(L):
    t = jnp.arange(L, dtype=jnp.int32)
    inv = ROPE_THETA ** (-jnp.arange(0, ROPE_AXIS_DIM, 2, dtype=jnp.float32) / ROPE_AXIS_DIM)
    ang_r = (t // GRID_W).astype(jnp.float32)[:, None] * inv
    ang_c = (t % GRID_W).astype(jnp.float32)[:, None] * inv
    return (jnp.cos(ang_r), jnp.sin(ang_r), jnp.cos(ang_c), jnp.sin(ang_c))


def _rotate(x, cos, sin):
    x1, x2 = jnp.split(x, 2, axis=-1)
    cos = cos[:, None, :].astype(x.dtype)
    sin = sin[:, None, :].astype(x.dtype)
    return jnp.concatenate([x1 * cos - x2 * sin, x1 * sin + x2 * cos], axis=-1)


def apply_axial_rope(x, tabs):
    cos_r, sin_r, cos_c, sin_c = tabs
    x_row, x_col = jnp.split(x, 2, axis=-1)
    return jnp.concatenate([_rotate(x_row, cos_r, sin_r), _rotate(x_col, cos_c, sin_c)], axis=-1)


def project(h, w):
    z = h @ w
    B, L, _ = z.shape
    points, acc = [], 0
    for s in IN_SIZES[:-1]:
        acc += s
        points.append(acc)
    parts = jnp.split(z, points, axis=-1)
    pool_in = parts[0]
    q_na, k_na, v_na = [t.reshape(B, L, NA_HEADS, HEAD_DIM) for t in parts[1:4]]
    q_g = parts[4].reshape(B, L, GQA_Q_HEADS, HEAD_DIM)
    k_g, v_g = [t.reshape(B, L, GQA_KV_HEADS, HEAD_DIM) for t in parts[5:7]]
    return pool_in, q_na, k_na, v_na, q_g, k_g, v_g


def multiscale_pool(u, w_grp, scale):
    B, L, _ = u.shape
    ug = u.reshape(B, L, N_POOL, POOL_DIM)
    csum = jnp.concatenate([jnp.zeros((B, 1, N_POOL, POOL_DIM), jnp.float32),
                            jnp.cumsum(ug.astype(jnp.float32), axis=1)], axis=1)
    t = jnp.arange(L)
    means = []
    for g, w in enumerate(POOL_WINDOWS):
        lo = jnp.clip(t - w // 2, 0, L - 1)
        hi = jnp.clip(t + w // 2 - 1, 0, L - 1)
        cnt = (hi - lo + 1).astype(jnp.float32)[None, :, None]
        means.append((csum[:, hi + 1, g] - csum[:, lo, g]) / cnt)
    pooled = jnp.stack(means, axis=2).astype(u.dtype) - ug
    y = jnp.einsum('blgc,gcd->blgd', pooled, w_grp)
    return y.reshape(B, L, N_POOL * POOL_DIM) * scale


def attend(q, k, v):
    B, Lq, H, dh = q.shape
    hkv = k.shape[2]
    qg = q.reshape(B, Lq, hkv, H // hkv, dh)
    s = jnp.einsum('bqkgd,bskd->bkgqs', qg, k, preferred_element_type=jnp.float32) * (dh ** -0.5)
    p = jax.nn.softmax(s, axis=-1).astype(v.dtype)
    return jnp.einsum('bkgqs,bskd->bqkgd', p, v).reshape(B, Lq, H * dh)


def blocked_gqa(q, k, v, k_ctx, v_ctx):
    B, L, H, dh = q.shape
    k_all = jnp.concatenate([k_ctx, k], axis=1)
    v_all = jnp.concatenate([v_ctx, v], axis=1)
    qb = q.reshape(B, L // Q_BLOCK, Q_BLOCK, H, dh).swapaxes(0, 1)
    out = lax.map(lambda q_blk: attend(q_blk, k_all, v_all), qb)
    return out.swapaxes(0, 1).reshape(B, L, H * dh)


def neighbourhood_attention(q, k, v, k_ctx, v_ctx, rpb):
    B, L, H, dh = q.shape
    rows = L // GRID_W
    kh = min(NA_ROWS, rows)
    n_nb = kh * NA_COLS
    qg = q.reshape(B, rows, GRID_W, H, dh) * (dh ** -0.5)
    kg = k.reshape(B, rows, GRID_W, H, dh)
    vg = v.reshape(B, rows, GRID_W, H, dh)
    col = jnp.arange(GRID_W)
    c_start = jnp.clip(col - NA_COLS // 2, 0, GRID_W - NA_COLS)
    c_idx = c_start[:, None] + jnp.arange(NA_COLS)
    dc = c_idx - col[:, None] + (NA_COLS - 1)
    rpb_c = rpb[:, :, dc].astype(jnp.float32)

    def one_row(r):
        r_start = jnp.clip(r - kh // 2, 0, rows - kh)
        k_band = lax.dynamic_slice_in_dim(kg, r_start, kh, axis=1)
        v_band = lax.dynamic_slice_in_dim(vg, r_start, kh, axis=1)
        k_nb = k_band[:, :, c_idx]
        v_nb = v_band[:, :, c_idx]
        q_row = lax.dynamic_index_in_dim(qg, r, axis=1, keepdims=False)
        dr = r_start + jnp.arange(kh) - r + (NA_ROWS - 1)
        bias = rpb_c[:, dr].transpose(0, 2, 1, 3)
        s_nb = jnp.einsum('bqhd,bkqjhd->bhqkj', q_row, k_nb,
                          preferred_element_type=jnp.float32) + bias[None]
        s_ctx = jnp.einsum('bqhd,bchd->bhqc', q_row, k_ctx, preferred_element_type=jnp.float32)
        s = jnp.concatenate([s_nb.reshape(B, H, GRID_W, n_nb), s_ctx], axis=-1)
        p = jax.nn.softmax(s, axis=-1).astype(v.dtype)
        p_nb = p[..., :n_nb].reshape(B, H, GRID_W, kh, NA_COLS)
        p_ctx = p[..., n_nb:]
        return (jnp.einsum('bhqkj,bkqjhd->bqhd', p_nb, v_nb)
                + jnp.einsum('bhqc,bchd->bqhd', p_ctx, v_ctx))

    out = lax.map(one_row, jnp.arange(rows))
    return out.transpose(1, 0, 2, 3, 4).reshape(B, L, H * dh)


def conv_ffn(h, w_up, conv_w, conv_b, w_down):
    u = h @ w_up
    L = u.shape[1]
    half = CONV_WIDTH // 2
    up = jnp.pad(u, ((0, 0), (half, half), (0, 0)))
    acc = up[:, 0:L] * conv_w[0] + conv_b
    for j in range(1, CONV_WIDTH):
        acc = acc + up[:, j:j + L] * conv_w[j]
    a, g = jnp.split(acc, 2, axis=-1)
    return (a * jax.nn.silu(g)) @ w_down


def setup_inputs(seed: int = 0) -> dict:
    key = jax.random.key(seed)
    ks = jax.random.split(key, 22)

    def nrm(k, shape, s):
        return jax.random.normal(k, shape, jnp.float32) * s

    return {
        'x': nrm(ks[0], (BATCH, SEQ, D_MODEL), 1.0),
        'c': nrm(ks[1], (BATCH, D_MODEL), 1.0),
        'ctx': nrm(ks[2], (BATCH, CTX_LEN, D_MODEL), 1.0),
        'c_ctx': nrm(ks[3], (D_MODEL,), 1.0),
        'w_mod': nrm(ks[4], (DEPTH, D_MODEL, 6 * D_MODEL), 0.5 * D_MODEL ** -0.5),
        'b_mod': nrm(ks[5], (DEPTH, 6 * D_MODEL), 0.02),
        'w_in': nrm(ks[6], (DEPTH, D_MODEL, IN_WIDTH), D_MODEL ** -0.5),
        'pool_w': nrm(ks[7], (DEPTH, N_POOL, POOL_DIM, POOL_DIM), POOL_DIM ** -0.5),
        'pool_scale': 1.0 + nrm(ks[8], (DEPTH, POOL_WIDTH), 0.1),
        'na_rpb': nrm(ks[9], (DEPTH, NA_HEADS, 2 * NA_ROWS - 1, 2 * NA_COLS - 1), 0.1),
        'q_norm': 1.0 + nrm(ks[10], (DEPTH, HEAD_DIM), 0.1),
        'k_norm': 1.0 + nrm(ks[11], (DEPTH, HEAD_DIM), 0.1),
        'w_out': nrm(ks[12], (DEPTH, MIX_WIDTH, D_MODEL), MIX_WIDTH ** -0.5 * DEEPNORM_BETA),
        'ln1_g': 1.0 + nrm(ks[13], (DEPTH, D_MODEL), 0.1),
        'ln1_b': nrm(ks[14], (DEPTH, D_MODEL), 0.02),
        'w_up': nrm(ks[15], (DEPTH, D_MODEL, 2 * D_FF), D_MODEL ** -0.5),
        'conv_w': nrm(ks[16], (DEPTH, CONV_WIDTH, 2 * D_FF), CONV_WIDTH ** -0.5),
        'conv_b': nrm(ks[17], (DEPTH, 2 * D_FF), 0.02),
        'w_down': nrm(ks[18], (DEPTH, D_FF, D_MODEL), D_FF ** -0.5 * DEEPNORM_BETA),
        'ln2_g': 1.0 + nrm(ks[19], (DEPTH, D_MODEL), 0.1),
        'ln2_b': nrm(ks[20], (DEPTH, D_MODEL), 0.02),
    }


def reference(x, c, ctx, c_ctx, w_mod, b_mod, w_in, pool_w, pool_scale, na_rpb, q_norm, k_norm,
              w_out, ln1_g, ln1_b, w_up, conv_w, conv_b, w_down, ln2_g, ln2_b):
    rope = axial_rope_tables(x.shape[1])
    xc = ctx
    act_lat = jax.nn.silu(c)
    act_ctx = jax.nn.silu(c_ctx)
    for l in range(DEPTH):
        mod = act_lat @ w_mod[l] + b_mod[l]
        mod_c = act_ctx @ w_mod[l] + b_mod[l]
        sh1, s1, g1, sh2, s2, g2 = [m[:, None, :] for m in jnp.split(mod, 6, axis=-1)]
        sh1c, s1c, g1c, sh2c, s2c, g2c = jnp.split(mod_c, 6, axis=-1)

        p_c, qn_c, kn_c, vn_c, qg_c, kg_c, vg_c = project(xc * (1 + s1c) + sh1c, w_in[l])
        kg_c = rms_norm(kg_c, k_norm[l])

        p, qn, kn, vn, qg, kg, vg = project(x * (1 + s1) + sh1, w_in[l])
        y_pool = multiscale_pool(p, pool_w[l], pool_scale[l])
        y_na = neighbourhood_attention(qn, kn, vn, kn_c, vn_c, na_rpb[l])
        qg = apply_axial_rope(rms_norm(qg, q_norm[l]), rope)
        kg = apply_axial_rope(rms_norm(kg, k_norm[l]), rope)
        y_gqa = blocked_gqa(qg, kg, vg, kg_c, vg_c)
        mix = jnp.concatenate([y_pool, y_na, y_gqa], axis=-1) @ w_out[l]
        x = layer_norm(DEEPNORM_ALPHA * x + g1 * mix, ln1_g[l], ln1_b[l])
        ffn = conv_ffn(x * (1 + s2) + sh2, w_up[l], conv_w[l], conv_b[l], w_down[l])
        x = layer_norm(DEEPNORM_ALPHA * x + g2 * ffn, ln2_g[l], ln2_b[l])

        if l < DEPTH - 1:
            mix_c = jnp.concatenate([
                multiscale_pool(p_c, pool_w[l], pool_scale[l]),
                attend(qn_c, kn_c, vn_c),
                attend(rms_norm(qg_c, q_norm[l]), kg_c, vg_c)], axis=-1) @ w_out[l]
            xc = layer_norm(DEEPNORM_ALPHA * xc + g1c * mix_c, ln1_g[l], ln1_b[l])
            ffn_c = conv_ffn(xc * (1 + s2c) + sh2c, w_up[l], conv_w[l], conv_b[l], w_down[l])
            xc = layer_norm(DEEPNORM_ALPHA * xc + g2c * ffn_c, ln2_g[l], ln2_b[l])
    return x
```

```python
import functools

import jax
import jax.numpy as jnp
from jax import lax
from jax.experimental import pallas as pl
from jax.experimental.pallas import tpu as pltpu

F32 = jnp.float32
BF16 = jnp.bfloat16

D_MODEL = 1024
GRID_W = 64
HEAD_DIM = 64
LANES = 128
POOL_WIDTH = 256
POOL_WINDOWS = (2, 4, 8, 16)
POOL_HALO = 8
NA_WIDTH = 384
NA_HEADS = 6
NA_ROWS = 8
NA_COLS = 16
GQA_WIDTH = 384
GQA_KV_WIDTH = 128
D_FF = 2816
FF_CHUNK = 1408
ROPE_THETA = 10000.0
DEEPNORM_ALPHA = 8.0 ** 0.25
LN_EPS = 1e-6
MASK_VALUE = -1e30
VMEM_LIMIT = 56 * 1024 * 1024

_C_POOL = 0
_C_NA = POOL_WIDTH
_C_GQ = _C_NA + 3 * NA_WIDTH
_C_GK = _C_GQ + GQA_WIDTH
_C_GV = _C_GK + GQA_KV_WIDTH
IN_WIDTH = _C_GV + GQA_KV_WIDTH


def _params(n_axes):
    return pltpu.CompilerParams(dimension_semantics=("parallel",) * n_axes,
                                vmem_limit_bytes=VMEM_LIMIT)


def _const_spec(shape):
    return pl.BlockSpec(shape, lambda *_: (0,) * len(shape), pipeline_mode=pl.Buffered(1))


def _dot(a, b):
    return jnp.dot(a, b, preferred_element_type=F32)


def _dot_nt(a, b):
    return lax.dot_general(a, b, (((1,), (1,)), ((), ())), preferred_element_type=F32)


def _layer_norm(y, g, b):
    mu = jnp.mean(y, axis=-1, keepdims=True)
    d = y - mu
    var = jnp.mean(d * d, axis=-1, keepdims=True)
    return d * lax.rsqrt(var + LN_EPS) * g + b


def _mod_kernel(c_ref, w_ref, b_ref, o_ref):
    c = c_ref[...]
    act = c * jax.nn.sigmoid(c)
    o_ref[0] = _dot(act, w_ref[0]) + b_ref[0]


def _modulation(c_rows, w_mod, b_mod):
    depth, d, n = w_mod.shape
    rows = c_rows.shape[0]
    bn = 1536
    return pl.pallas_call(
        _mod_kernel,
        out_shape=jax.ShapeDtypeStruct((depth, rows, n), F32),
        grid=(depth, n // bn),
        in_specs=[pl.BlockSpec((rows, d), lambda l, j: (0, 0)),
                  pl.BlockSpec((1, d, bn), lambda l, j: (l, 0, j)),
                  pl.BlockSpec((1, 1, bn), lambda l, j: (l, 0, j))],
        out_specs=pl.BlockSpec((1, rows, bn), lambda l, j: (l, 0, j)),
        compiler_params=_params(2),
        name="modulation",
    )(c_rows, w_mod, b_mod.reshape(depth, 1, n))


def _swap16(v):
    lane = lax.broadcasted_iota(jnp.int32, v.shape, 1)
    even = jnp.bitwise_and(lane, 16) == 0
    return jnp.where(even, pltpu.roll(v, LANES - 16, 1), pltpu.roll(v, 16, 1))


def _head_sumsq(z):
    r = jnp.right_shift(lax.broadcasted_iota(jnp.int32, (LANES, LANES), 0), 6)
    c = jnp.right_shift(lax.broadcasted_iota(jnp.int32, (LANES, LANES), 1), 6)
    ones_blk = (r == c).astype(BF16)
    sq = z * z
    hi = sq.astype(BF16)
    lo = (sq - hi.astype(F32)).astype(BF16)
    return _dot(hi, ones_blk) + _dot(lo, ones_blk)


def _norm_rope(z, gain, cos, sin):
    zn = z * lax.rsqrt(_head_sumsq(z) * (1.0 / HEAD_DIM) + LN_EPS) * gain
    return zn * cos + _swap16(zn) * sin


def _inproj_kernel(x_ref, sh_ref, sc_ref, w_ref, cos_ref, sin_ref, qg_ref, kg_ref,
                   pool_ref, na_ref, q_ref, k_ref, v_ref):
    h = (x_ref[...] * (1.0 + sc_ref[0]) + sh_ref[0]).astype(BF16)
    pool_ref[...] = _dot(h, w_ref[:, _C_POOL:_C_NA])
    na_ref[:, 0:NA_WIDTH] = (_dot(h, w_ref[:, _C_NA:_C_NA + NA_WIDTH]) * HEAD_DIM ** -0.5).astype(BF16)
    na_ref[:, NA_WIDTH:3 * NA_WIDTH] = _dot(h, w_ref[:, _C_NA + NA_WIDTH:_C_GQ]).astype(BF16)
    cos = cos_ref[...]
    sin = sin_ref[...]
    zq = _dot(h, w_ref[:, _C_GQ:_C_GK])
    for j in range(GQA_WIDTH // LANES):
        blk = _norm_rope(zq[:, j * LANES:(j + 1) * LANES], qg_ref[...], cos, sin)
        q_ref[:, j * LANES:(j + 1) * LANES] = (blk * HEAD_DIM ** -0.5).astype(BF16)
    zk = _dot(h, w_ref[:, _C_GK:_C_GV])
    k_ref[...] = _norm_rope(zk, kg_ref[...], cos, sin).astype(BF16)
    v_ref[...] = _dot(h, w_ref[:, _C_GV:IN_WIDTH]).astype(BF16)


def _inproj(x2d, mod, mod_row, w_in, cos, sin, qg, kg, seq_len, tm):
    n = x2d.shape[0]
    tiles = seq_len // tm
    tok = lambda i: (i, 0)
    outs = (jax.ShapeDtypeStruct((n, POOL_WIDTH), F32),
            jax.ShapeDtypeStruct((n, 3 * NA_WIDTH), BF16),
            jax.ShapeDtypeStruct((n, GQA_WIDTH), BF16),
            jax.ShapeDtypeStruct((n, GQA_KV_WIDTH), BF16),
            jax.ShapeDtypeStruct((n, GQA_KV_WIDTH), BF16))
    return pl.pallas_call(
        _inproj_kernel,
        out_shape=outs,
        grid=(n // tm,),
        in_specs=[pl.BlockSpec((tm, D_MODEL), tok),
                  pl.BlockSpec((1, 1, D_MODEL), lambda i: (mod_row(i // tiles), 0, 0)),
                  pl.BlockSpec((1, 1, D_MODEL), lambda i: (mod_row(i // tiles), 0, 1)),
                  _const_spec((D_MODEL, IN_WIDTH)),
                  pl.BlockSpec((tm, LANES), lambda i: (i % tiles, 0)),
                  pl.BlockSpec((tm, LANES), lambda i: (i % tiles, 0)),
                  _const_spec((1, LANES)),
                  _const_spec((1, LANES))],
        out_specs=[pl.BlockSpec((tm, POOL_WIDTH), tok),
                   pl.BlockSpec((tm, 3 * NA_WIDTH), tok),
                   pl.BlockSpec((tm, GQA_WIDTH), tok),
                   pl.BlockSpec((tm, GQA_KV_WIDTH), tok),
                   pl.BlockSpec((tm, GQA_KV_WIDTH), tok)],
        compiler_params=_params(1),
        name="inproj",
    )(x2d, mod, mod, w_in, cos, sin, qg, kg)


def _gqa_kernel(q_ref, k_ref, v_ref, o_ref, *, tq, tk):
    n_k = k_ref.shape[1] // tk
    lane = lax.broadcasted_iota(jnp.int32, (tq, LANES), 1)
    low = lane < HEAD_DIM
    zero = jnp.zeros((tq, LANES), BF16)
    parts = []
    for grp in range(2):
        for j in range(3):
            blk = q_ref[:, j * LANES:(j + 1) * LANES]
            parts.append(jnp.where(low if grp == 0 else jnp.logical_not(low), blk, zero))
    qx = jnp.concatenate(parts, axis=0)
    rows = 6 * tq

    def body(i, carry):
        m, l, acc = carry
        start = pl.multiple_of(i * tk, tk)
        kc = k_ref[0, pl.ds(start, tk), :]
        vc = v_ref[0, pl.ds(start, tk), :]
        s = _dot_nt(qx, kc)
        m_new = jnp.maximum(m, jnp.max(s, axis=1, keepdims=True))
        p = jnp.exp(s - m_new)
        a = jnp.exp(m - m_new)
        l = a * l + jnp.sum(p, axis=1, keepdims=True)
        acc = a * acc + _dot(p.astype(BF16), vc)
        return m_new, l, acc

    m0 = jnp.full((rows, 1), MASK_VALUE, F32)
    l0 = jnp.zeros((rows, 1), F32)
    acc0 = jnp.zeros((rows, LANES), F32)
    _, l, acc = lax.fori_loop(0, n_k, body, (m0, l0, acc0))
    out = acc / l
    for j in range(3):
        o_ref[:, j * LANES:(j + 1) * LANES] = jnp.where(
            low, out[j * tq:(j + 1) * tq], out[(3 + j) * tq:(4 + j) * tq]).astype(BF16)


def _gqa(q2d, k3d, v3d, seq_len, tq, tk):
    n = q2d.shape[0]
    lk = k3d.shape[1]
    tiles = seq_len // tq
    return pl.pallas_call(
        functools.partial(_gqa_kernel, tq=tq, tk=tk),
        out_shape=jax.ShapeDtypeStruct((n, GQA_WIDTH), BF16),
        grid=(n // tq,),
        in_specs=[pl.BlockSpec((tq, GQA_WIDTH), lambda i: (i, 0)),
                  pl.BlockSpec((1, lk, GQA_KV_WIDTH), lambda i: (i // tiles, 0, 0)),
                  pl.BlockSpec((1, lk, GQA_KV_WIDTH), lambda i: (i // tiles, 0, 0))],
        out_specs=pl.BlockSpec((tq, GQA_WIDTH), lambda i: (i, 0)),
        compiler_params=_params(1),
        name="gqa",
    )(q2d, k3d, v3d)


def _pair_queries(qb, low):
    zero = jnp.zeros_like(qb)
    return jnp.concatenate([jnp.where(low, qb, zero), jnp.where(low, zero, qb)], axis=0)


def _ctx_mha_kernel(q_ref, k_ref, v_ref, o_ref):
    rows = q_ref.shape[0]
    lane = lax.broadcasted_iota(jnp.int32, (rows, LANES), 1)
    low = lane < HEAD_DIM
    for j in range(NA_WIDTH // LANES):
        cols = slice(j * LANES, (j + 1) * LANES)
        qx = _pair_queries(q_ref[:, cols], low)
        s = _dot_nt(qx, k_ref[:, cols])
        p = jnp.exp(s - jnp.max(s, axis=1, keepdims=True))
        l = jnp.sum(p, axis=1, keepdims=True)
        o = _dot(p.astype(BF16), v_ref[:, cols]) / l
        o_ref[:, cols] = jnp.where(low, o[:rows], o[rows:]).astype(BF16)


def _ctx_mha(na_c, ctx_len):
    n = na_c.shape[0]
    return pl.pallas_call(
        _ctx_mha_kernel,
        out_shape=jax.ShapeDtypeStruct((n, NA_WIDTH), BF16),
        grid=(n // ctx_len,),
        in_specs=[pl.BlockSpec((ctx_len, NA_WIDTH), lambda b: (b, 0)),
                  pl.BlockSpec((ctx_len, NA_WIDTH), lambda b: (b, 1)),
                  pl.BlockSpec((ctx_len, NA_WIDTH), lambda b: (b, 2))],
        out_specs=pl.BlockSpec((ctx_len, NA_WIDTH), lambda b: (b, 0)),
        compiler_params=_params(1),
        name="ctx_mha",
    )(na_c, na_c, na_c)


NA_BLOCK_ROWS = 8
NA_BLOCK = NA_BLOCK_ROWS * GRID_W
NA_BAND = NA_ROWS * GRID_W


def _na_kernel(q_ref, kp_ref, kc_ref, kn_ref, vp_ref, vc_ref, vn_ref, kx_ref, vx_ref, bias_ref,
               o_ref, kband, vband, *, n_rows):
    i = pl.program_id(1)
    kband[0:NA_BLOCK] = kp_ref[...]
    kband[NA_BLOCK:2 * NA_BLOCK] = kc_ref[...]
    kband[2 * NA_BLOCK:3 * NA_BLOCK] = kn_ref[...]
    vband[0:NA_BLOCK] = vp_ref[...]
    vband[NA_BLOCK:2 * NA_BLOCK] = vc_ref[...]
    vband[2 * NA_BLOCK:3 * NA_BLOCK] = vn_ref[...]
    lane = lax.broadcasted_iota(jnp.int32, (GRID_W, LANES), 1)
    low = lane < HEAD_DIM

    def row_body(j, carry):
        r = i * NA_BLOCK_ROWS + j
        r_start = jnp.clip(r - NA_ROWS // 2, 0, n_rows - NA_ROWS)
        shift = r - r_start
        off = pl.multiple_of((r_start - (i - 1) * NA_BLOCK_ROWS) * GRID_W, GRID_W)
        qrow = pl.multiple_of(j * GRID_W, GRID_W)
        for p in range(NA_WIDTH // LANES):
            cols = slice(p * LANES, (p + 1) * LANES)
            qx = _pair_queries(q_ref[pl.ds(qrow, GRID_W), cols], low)
            s_nb = _dot_nt(qx, kband[pl.ds(off, NA_BAND), cols]) + bias_ref[p, shift]
            s_cx = _dot_nt(qx, kx_ref[:, cols])
            m = jnp.maximum(jnp.max(s_nb, axis=1, keepdims=True), jnp.max(s_cx, axis=1, keepdims=True))
            p_nb = jnp.exp(s_nb - m)
            p_cx = jnp.exp(s_cx - m)
            l = jnp.sum(p_nb, axis=1, keepdims=True) + jnp.sum(p_cx, axis=1, keepdims=True)
            o = (_dot(p_nb.astype(BF16), vband[pl.ds(off, NA_BAND), cols])
                 + _dot(p_cx.astype(BF16), vx_ref[:, cols])) / l
            o_ref[pl.ds(qrow, GRID_W), cols] = jnp.where(low, o[:GRID_W], o[GRID_W:]).astype(BF16)
        return carry

    lax.fori_loop(0, NA_BLOCK_ROWS, row_body, 0)


def _na_bias_table(rpb):
    col = jnp.arange(GRID_W)
    c_start = jnp.clip(col - NA_COLS // 2, 0, GRID_W - NA_COLS)
    in_win = (col[None, :] >= c_start[:, None]) & (col[None, :] < c_start[:, None] + NA_COLS)
    dc = jnp.clip(col[None, :] - col[:, None] + NA_COLS - 1, 0, 2 * NA_COLS - 2)
    shift = jnp.arange(NA_ROWS)
    dr = jnp.arange(NA_ROWS)[None, :] - shift[:, None] + NA_ROWS - 1
    t = rpb[:, dr[:, :, None, None], dc[None, None, :, :]]
    t = jnp.where(in_win[None, None, None], t.astype(F32), MASK_VALUE)
    t = t.transpose(0, 1, 3, 2, 4).reshape(NA_HEADS // 2, 2, NA_ROWS, GRID_W, NA_BAND)
    return t.transpose(0, 2, 1, 3, 4).reshape(NA_HEADS // 2, NA_ROWS, 2 * GRID_W, NA_BAND)


def _na(na, na_c, bias, batch, seq_len, ctx_len):
    n = na.shape[0]
    n_rows = seq_len // GRID_W
    blocks = seq_len // NA_BLOCK
    tok = lambda col: (lambda b, i: (b * blocks + i, col))
    prev = lambda col: (lambda b, i: (b * blocks + jnp.maximum(i - 1, 0), col))
    nxt = lambda col: (lambda b, i: (b * blocks + jnp.minimum(i + 1, blocks - 1), col))
    blk = lambda f: pl.BlockSpec((NA_BLOCK, NA_WIDTH), f)
    return pl.pallas_call(
        functools.partial(_na_kernel, n_rows=n_rows),
        out_shape=jax.ShapeDtypeStruct((n, NA_WIDTH), BF16),
        grid=(batch, blocks),
        in_specs=[blk(tok(0)),
                  blk(prev(1)), blk(tok(1)), blk(nxt(1)),
                  blk(prev(2)), blk(tok(2)), blk(nxt(2)),
                  pl.BlockSpec((ctx_len, NA_WIDTH), lambda b, i: (b, 1)),
                  pl.BlockSpec((ctx_len, NA_WIDTH), lambda b, i: (b, 2)),
                  _const_spec(bias.shape)],
        out_specs=blk(tok(0)),
        scratch_shapes=[pltpu.VMEM((3 * NA_BLOCK, NA_WIDTH), BF16),
                        pltpu.VMEM((3 * NA_BLOCK, NA_WIDTH), BF16)],
        compiler_params=_params(2),
        name="natten",
    )(na, na, na, na, na, na, na, na_c, na_c, bias)


def _outproj_kernel(pp_ref, pm_ref, pn_ref, yna_ref, ygq_ref, x_ref, g1_ref, wout_ref, pw_ref, ps_ref,
                    lng_ref, lnb_ref, o_ref, *, seq_len, tm):
    tiles = seq_len // tm
    ti = pl.program_id(0) % tiles
    prev = jnp.where(ti == 0, 0.0, pp_ref[...])
    nxt = jnp.where(ti == tiles - 1, 0.0, pn_ref[...])
    u = jnp.concatenate([prev, pm_ref[...], nxt], axis=0)
    n = tm + 2 * POOL_HALO
    s2 = u + pltpu.roll(u, 1, 0)
    s4 = pltpu.roll(s2, 1, 0) + pltpu.roll(s2, n - 1, 0)
    s8 = pltpu.roll(s4, 2, 0) + pltpu.roll(s4, n - 2, 0)
    s16 = pltpu.roll(s8, 4, 0) + pltpu.roll(s8, n - 4, 0)
    mid = slice(POOL_HALO, POOL_HALO + tm)
    pos = ti * tm + lax.broadcasted_iota(jnp.int32, (tm, POOL_WIDTH), 0)
    grp = jnp.right_shift(lax.broadcasted_iota(jnp.int32, (tm, POOL_WIDTH), 1), 6)
    half = jnp.left_shift(1, grp)
    cnt = jnp.minimum(pos + half - 1, seq_len - 1) - jnp.maximum(pos - half, 0) + 1
    wsum = jnp.where(grp == 0, s2[mid], jnp.where(grp == 1, s4[mid], jnp.where(grp == 2, s8[mid], s16[mid])))
    pooled = wsum / cnt.astype(F32) - u[mid]
    y_pool = _dot(pooled.astype(BF16), pw_ref[...]) * ps_ref[...]
    mix = (_dot(y_pool.astype(BF16), wout_ref[0:POOL_WIDTH])
           + _dot(yna_ref[...], wout_ref[POOL_WIDTH:POOL_WIDTH + NA_WIDTH])
           + _dot(ygq_ref[...], wout_ref[POOL_WIDTH + NA_WIDTH:D_MODEL]))
    y = DEEPNORM_ALPHA * x_ref[...] + g1_ref[0] * mix
    o_ref[...] = _layer_norm(y, lng_ref[...], lnb_ref[...])


def _outproj(pool_in, y_na, y_gqa, x2d, mod, mod_row, w_out, pool_w_bd, pool_scale, ln_g, ln_b, seq_len, tm):
    n = x2d.shape[0]
    tiles = seq_len // tm
    hb = tm // POOL_HALO
    last_hb = n // POOL_HALO - 1
    tok = lambda i: (i, 0)
    return pl.pallas_call(
        functools.partial(_outproj_kernel, seq_len=seq_len, tm=tm),
        out_shape=jax.ShapeDtypeStruct((n, D_MODEL), F32),
        grid=(n // tm,),
        in_specs=[pl.BlockSpec((POOL_HALO, POOL_WIDTH), lambda i: (jnp.maximum(i * hb - 1, 0), 0)),
                  pl.BlockSpec((tm, POOL_WIDTH), tok),
                  pl.BlockSpec((POOL_HALO, POOL_WIDTH), lambda i: (jnp.minimum((i + 1) * hb, last_hb), 0)),
                  pl.BlockSpec((tm, NA_WIDTH), tok),
                  pl.BlockSpec((tm, GQA_WIDTH), tok),
                  pl.BlockSpec((tm, D_MODEL), tok),
                  pl.BlockSpec((1, 1, D_MODEL), lambda i: (mod_row(i // tiles), 0, 2)),
                  _const_spec((D_MODEL, D_MODEL)),
                  _const_spec((POOL_WIDTH, POOL_WIDTH)),
                  _const_spec((1, POOL_WIDTH)),
                  _const_spec((1, D_MODEL)),
                  _const_spec((1, D_MODEL))],
        out_specs=pl.BlockSpec((tm, D_MODEL), tok),
        compiler_params=_params(1),
        name="outproj",
    )(pool_in, pool_in, pool_in, y_na, y_gqa, x2d, mod, w_out, pool_w_bd, pool_scale, ln_g, ln_b)


FFN_HALO = 8


def _ffn_kernel(xp_ref, xm_ref, xn_ref, sh_ref, sc_ref, g2_ref, wup_ref, cw_ref, cb_ref, wdn_ref,
                lng_ref, lnb_ref, o_ref, *, seq_len, tm):
    tiles = seq_len // tm
    ti = pl.program_id(0) % tiles
    scale = 1.0 + sc_ref[0]
    shift = sh_ref[0]
    xm = xm_ref[...]
    hp = jnp.where(ti == 0, 0.0, xp_ref[...] * scale + shift)
    hn = jnp.where(ti == tiles - 1, 0.0, xn_ref[...] * scale + shift)
    h = jnp.concatenate([hp, xm * scale + shift, hn], axis=0).astype(BF16)
    n = tm + 2 * FFN_HALO
    mid = slice(FFN_HALO, FFN_HALO + tm)

    def conv(u, c0):
        cw = cw_ref[:, c0:c0 + FF_CHUNK]
        y = (pltpu.roll(u, 1, 0) * cw[0:1] + cb_ref[:, c0:c0 + FF_CHUNK]
             + u * cw[1:2] + pltpu.roll(u, n - 1, 0) * cw[2:3])
        return y[mid]

    acc = jnp.zeros((tm, D_MODEL), F32)
    for ch in range(D_FF // FF_CHUNK):
        a0 = ch * FF_CHUNK
        g0 = D_FF + a0
        a = conv(_dot(h, wup_ref[:, a0:a0 + FF_CHUNK]), a0)
        g = conv(_dot(h, wup_ref[:, g0:g0 + FF_CHUNK]), g0)
        act = (a * (g * jax.nn.sigmoid(g))).astype(BF16)
        acc = acc + _dot(act, wdn_ref[a0:a0 + FF_CHUNK])
    y = DEEPNORM_ALPHA * xm + g2_ref[0] * acc
    o_ref[...] = _layer_norm(y, lng_ref[...], lnb_ref[...])


def _ffn(x2d, mod, mod_row, w_up, conv_w, conv_b, w_down, ln_g, ln_b, seq_len, tm):
    n = x2d.shape[0]
    tiles = seq_len // tm
    hb = tm // FFN_HALO
    last_hb = n // FFN_HALO - 1
    tok = lambda i: (i, 0)
    mod_spec = lambda k: pl.BlockSpec((1, 1, D_MODEL), lambda i: (mod_row(i // tiles), 0, k))
    return pl.pallas_call(
        functools.partial(_ffn_kernel, seq_len=seq_len, tm=tm),
        out_shape=jax.ShapeDtypeStruct((n, D_MODEL), F32),
        grid=(n // tm,),
        in_specs=[pl.BlockSpec((FFN_HALO, D_MODEL), lambda i: (jnp.maximum(i * hb - 1, 0), 0)),
                  pl.BlockSpec((tm, D_MODEL), tok),
                  pl.BlockSpec((FFN_HALO, D_MODEL), lambda i: (jnp.minimum((i + 1) * hb, last_hb), 0)),
                  mod_spec(3), mod_spec(4), mod_spec(5),
                  _const_spec((D_MODEL, 2 * D_FF)),
                  _const_spec((3, 2 * D_FF)),
                  _const_spec((1, 2 * D_FF)),
                  _const_spec((D_FF, D_MODEL)),
                  _const_spec((1, D_MODEL)),
                  _const_spec((1, D_MODEL))],
        out_specs=pl.BlockSpec((tm, D_MODEL), tok),
        compiler_params=_params(1),
        name="convffn",
    )(x2d, x2d, x2d, mod, mod, mod, w_up, conv_w, conv_b, w_down, ln_g, ln_b)


def _rope_tables(seq_len):
    t = jnp.arange(seq_len, dtype=jnp.int32)
    half = HEAD_DIM // 2
    inv = ROPE_THETA ** (-jnp.arange(0, half, 2, dtype=F32) / half)
    ang_r = (t // GRID_W).astype(F32)[:, None] * inv
    ang_c = (t % GRID_W).astype(F32)[:, None] * inv
    cos = jnp.concatenate([jnp.cos(ang_r)] * 2 + [jnp.cos(ang_c)] * 2, axis=-1)
    sin = jnp.concatenate([-jnp.sin(ang_r), jnp.sin(ang_r), -jnp.sin(ang_c), jnp.sin(ang_c)], axis=-1)
    return jnp.tile(cos, (1, 2)), jnp.tile(sin, (1, 2))


def _pair_head_order():
    heads = jnp.array([0, 3, 1, 4, 2, 5])
    return (heads[:, None] * HEAD_DIM + jnp.arange(HEAD_DIM)[None, :]).reshape(-1)


def kernel(x, c, ctx, c_ctx, w_mod, b_mod, w_in, pool_w, pool_scale, na_rpb, q_norm, k_norm, w_out,
           ln1_g, ln1_b, w_up, conv_w, conv_b, w_down, ln2_g, ln2_b):
    batch, seq_len, d = x.shape
    ctx_len = ctx.shape[1]
    depth = w_mod.shape[0]
    assert d == D_MODEL and seq_len % NA_BLOCK == 0 and seq_len // GRID_W >= NA_ROWS
    tm = 512
    tm_c = ctx_len
    tq = 256
    tk = 768 if (seq_len + ctx_len) % 768 == 0 else 256
    assert (seq_len + ctx_len) % tk == 0 and seq_len % tq == 0 and ctx_len % 256 == 0

    mod_rows = -(-(batch + 1) // 8) * 8
    c_rows = jnp.zeros((mod_rows, d), F32).at[:batch].set(c).at[batch].set(c_ctx)
    mods = _modulation(c_rows, w_mod, b_mod)
    lat_row = lambda b: b
    ctx_row = lambda b: batch

    order = _pair_head_order()
    q_cols = _C_GQ + order
    w_in_p = jnp.concatenate([w_in[:, :, :_C_GQ], w_in[:, :, q_cols], w_in[:, :, _C_GK:]], axis=-1).astype(BF16)
    o_rows = POOL_WIDTH + NA_WIDTH + order
    w_out_p = jnp.concatenate([w_out[:, :POOL_WIDTH + NA_WIDTH], w_out[:, o_rows]], axis=1).astype(BF16)
    w_up_b = w_up.astype(BF16)
    w_down_b = w_down.astype(BF16)
    eye = jnp.eye(pool_w.shape[1], dtype=F32)
    pool_w_bd = jnp.einsum("gh,lgcd->lgchd", eye, pool_w).reshape(depth, POOL_WIDTH, POOL_WIDTH).astype(BF16)

    cos, sin = _rope_tables(seq_len)
    cos_c = jnp.ones((ctx_len, LANES), F32)
    sin_c = jnp.zeros((ctx_len, LANES), F32)

    xl = x.reshape(batch * seq_len, d)
    xc = ctx.reshape(batch * ctx_len, d)
    for l in range(depth):
        mod = mods[l].reshape(mod_rows, 1, 6 * d)
        qg = jnp.tile(q_norm[l], 2).reshape(1, LANES)
        kg = jnp.tile(k_norm[l], 2).reshape(1, LANES)
        ps = pool_scale[l].reshape(1, POOL_WIDTH)
        g1, b1 = ln1_g[l].reshape(1, d), ln1_b[l].reshape(1, d)
        g2, b2 = ln2_g[l].reshape(1, d), ln2_b[l].reshape(1, d)
        cb = conv_b[l].reshape(1, 2 * D_FF)

        p_c, na_c, q_c, k_c, v_c = _inproj(xc, mod, ctx_row, w_in_p[l], cos_c, sin_c, qg, kg, ctx_len, tm_c)
        p_l, na_l, q_l, k_l, v_l = _inproj(xl, mod, lat_row, w_in_p[l], cos, sin, qg, kg, seq_len, tm)

        k_c3 = k_c.reshape(batch, ctx_len, GQA_KV_WIDTH)
        v_c3 = v_c.reshape(batch, ctx_len, GQA_KV_WIDTH)
        k_all = jnp.concatenate([k_c3, k_l.reshape(batch, seq_len, GQA_KV_WIDTH)], axis=1)
        v_all = jnp.concatenate([v_c3, v_l.reshape(batch, seq_len, GQA_KV_WIDTH)], axis=1)
        y_gqa = _gqa(q_l, k_all, v_all, seq_len, tq, tk)
        y_na = _na(na_l, na_c, _na_bias_table(na_rpb[l]), batch, seq_len, ctx_len)
        x1 = _outproj(p_l, y_na, y_gqa, xl, mod, lat_row, w_out_p[l], pool_w_bd[l], ps, g1, b1, seq_len, tm)
        xl = _ffn(x1, mod, lat_row, w_up_b[l], conv_w[l], cb, w_down_b[l], g2, b2, seq_len, tm)

        if l < depth - 1:
            y_na_c = _ctx_mha(na_c, ctx_len)
            y_gqa_c = _gqa(q_c, k_c3, v_c3, ctx_len, ctx_len, ctx_len)
            xc1 = _outproj(p_c, y_na_c, y_gqa_c, xc, mod, ctx_row, w_out_p[l], pool_w_bd[l], ps, g1, b1,
                           ctx_len, tm_c)
            xc = _ffn(xc1, mod, ctx_row, w_up_b[l], conv_w[l], cb, w_down_b[l], g2, b2, ctx_len, tm_c)
    return xl.reshape(batch, seq_len, d)
```

```python
import functools

import jax
import jax.numpy as jnp
from jax import lax
from jax.experimental import pallas as pl
from jax.experimental.pallas import tpu as pltpu

F32 = jnp.float32
BF16 = jnp.bfloat16

D_MODEL = 1024
GRID_W = 64
HEAD_DIM = 64
LANES = 128
POOL_WIDTH = 256
POOL_WINDOWS = (2, 4, 8, 16)
POOL_HALO = 8
NA_WIDTH = 384
NA_HEADS = 6
NA_ROWS = 8
NA_COLS = 16
GQA_WIDTH = 384
GQA_KV_WIDTH = 128
D_FF = 2816
FF_CHUNK = 1408
ROPE_THETA = 10000.0
DEEPNORM_ALPHA = 8.0 ** 0.25
LN_EPS = 1e-6
MASK_VALUE = -1e30
LOG2_E = 1.4426950408889634
VMEM_LIMIT = 56 * 1024 * 1024

_C_POOL = 0
_C_NA = POOL_WIDTH
_C_GQ = _C_NA + 3 * NA_WIDTH
_C_GK = _C_GQ + GQA_WIDTH
_C_GV = _C_GK + GQA_KV_WIDTH
IN_WIDTH = _C_GV + GQA_KV_WIDTH


def _params(n_axes):
    return pltpu.CompilerParams(dimension_semantics=("parallel",) * n_axes,
                                vmem_limit_bytes=VMEM_LIMIT)


def _const_spec(shape):
    return pl.BlockSpec(shape, lambda *_: (0,) * len(shape), pipeline_mode=pl.Buffered(1))


def _dot(a, b):
    return jnp.dot(a, b, preferred_element_type=F32)


def _dot_nt(a, b):
    return lax.dot_general(a, b, (((1,), (1,)), ((), ())), preferred_element_type=F32)


def _layer_norm(y, g, b):
    mu = jnp.mean(y, axis=-1, keepdims=True)
    d = y - mu
    var = jnp.mean(d * d, axis=-1, keepdims=True)
    return d * lax.rsqrt(var + LN_EPS) * g + b


def _mod_kernel(c_ref, w_ref, b_ref, o_ref):
    c = c_ref[...]
    act = c * jax.nn.sigmoid(c)
    o_ref[0] = _dot(act, w_ref[0]) + b_ref[0]


def _modulation(c_rows, w_mod, b_mod):
    depth, d, n = w_mod.shape
    rows = c_rows.shape[0]
    bn = 1536
    return pl.pallas_call(
        _mod_kernel,
        out_shape=jax.ShapeDtypeStruct((depth, rows, n), F32),
        grid=(depth, n // bn),
        in_specs=[pl.BlockSpec((rows, d), lambda l, j: (0, 0)),
                  pl.BlockSpec((1, d, bn), lambda l, j: (l, 0, j)),
                  pl.BlockSpec((1, 1, bn), lambda l, j: (l, 0, j))],
        out_specs=pl.BlockSpec((1, rows, bn), lambda l, j: (l, 0, j)),
        compiler_params=_params(2),
        name="modulation",
    )(c_rows, w_mod, b_mod.reshape(depth, 1, n))


def _swap16(v):
    lane = lax.broadcasted_iota(jnp.int32, v.shape, 1)
    even = jnp.bitwise_and(lane, 16) == 0
    return jnp.where(even, pltpu.roll(v, LANES - 16, 1), pltpu.roll(v, 16, 1))


def _head_sumsq(z):
    r = jnp.right_shift(lax.broadcasted_iota(jnp.int32, (LANES, LANES), 0), 6)
    c = jnp.right_shift(lax.broadcasted_iota(jnp.int32, (LANES, LANES), 1), 6)
    ones_blk = (r == c).astype(BF16)
    sq = z * z
    hi = sq.astype(BF16)
    lo = (sq - hi.astype(F32)).astype(BF16)
    return _dot(hi, ones_blk) + _dot(lo, ones_blk)


def _norm_rope(z, gain, cos, sin):
    zn = z * lax.rsqrt(_head_sumsq(z) * (1.0 / HEAD_DIM) + LN_EPS) * gain
    return zn * cos + _swap16(zn) * sin


def _inproj_kernel(x_ref, sh_ref, sc_ref, w_ref, cos_ref, sin_ref, qg_ref, kg_ref,
                   pool_ref, na_ref, q_ref, k_ref, v_ref):
    h = (x_ref[...] * (1.0 + sc_ref[0]) + sh_ref[0]).astype(BF16)
    pool_ref[...] = _dot(h, w_ref[:, _C_POOL:_C_NA])
    na_ref[:, 0:NA_WIDTH] = (_dot(h, w_ref[:, _C_NA:_C_NA + NA_WIDTH])
                             * (HEAD_DIM ** -0.5 * LOG2_E)).astype(BF16)
    na_ref[:, NA_WIDTH:3 * NA_WIDTH] = _dot(h, w_ref[:, _C_NA + NA_WIDTH:_C_GQ]).astype(BF16)
    cos = cos_ref[...]
    sin = sin_ref[...]
    zq = _dot(h, w_ref[:, _C_GQ:_C_GK])
    for j in range(GQA_WIDTH // LANES):
        blk = _norm_rope(zq[:, j * LANES:(j + 1) * LANES], qg_ref[...], cos, sin)
        q_ref[:, j * LANES:(j + 1) * LANES] = (blk * (HEAD_DIM ** -0.5 * LOG2_E)).astype(BF16)
    zk = _dot(h, w_ref[:, _C_GK:_C_GV])
    k_ref[...] = _norm_rope(zk, kg_ref[...], cos, sin).astype(BF16)
    v_ref[...] = _dot(h, w_ref[:, _C_GV:IN_WIDTH]).astype(BF16)


def _inproj(x2d, mod, mod_row, w_in, cos, sin, qg, kg, seq_len, tm):
    n = x2d.shape[0]
    tiles = seq_len // tm
    tok = lambda i: (i, 0)
    outs = (jax.ShapeDtypeStruct((n, POOL_WIDTH), F32),
            jax.ShapeDtypeStruct((n, 3 * NA_WIDTH), BF16),
            jax.ShapeDtypeStruct((n, GQA_WIDTH), BF16),
            jax.ShapeDtypeStruct((n, GQA_KV_WIDTH), BF16),
            jax.ShapeDtypeStruct((n, GQA_KV_WIDTH), BF16))
    return pl.pallas_call(
        _inproj_kernel,
        out_shape=outs,
        grid=(n // tm,),
        in_specs=[pl.BlockSpec((tm, D_MODEL), tok),
                  pl.BlockSpec((1, 1, D_MODEL), lambda i: (mod_row(i // tiles), 0, 0)),
                  pl.BlockSpec((1, 1, D_MODEL), lambda i: (mod_row(i // tiles), 0, 1)),
                  _const_spec((D_MODEL, IN_WIDTH)),
                  pl.BlockSpec((tm, LANES), lambda i: (i % tiles, 0)),
                  pl.BlockSpec((tm, LANES), lambda i: (i % tiles, 0)),
                  _const_spec((1, LANES)),
                  _const_spec((1, LANES))],
        out_specs=[pl.BlockSpec((tm, POOL_WIDTH), tok),
                   pl.BlockSpec((tm, 3 * NA_WIDTH), tok),
                   pl.BlockSpec((tm, GQA_WIDTH), tok),
                   pl.BlockSpec((tm, GQA_KV_WIDTH), tok),
                   pl.BlockSpec((tm, GQA_KV_WIDTH), tok)],
        compiler_params=_params(1),
        name="inproj",
    )(x2d, mod, mod, w_in, cos, sin, qg, kg)


def _gqa_kernel(q_ref, k_ref, v_ref, o_ref, *, tq, tk, bounded):
    n_k = k_ref.shape[1] // tk
    lane = lax.broadcasted_iota(jnp.int32, (tq, LANES), 1)
    low = lane < HEAD_DIM
    zero = jnp.zeros((tq, LANES), BF16)
    parts = []
    for grp in range(2):
        for j in range(3):
            blk = q_ref[:, j * LANES:(j + 1) * LANES]
            parts.append(jnp.where(low if grp == 0 else jnp.logical_not(low), blk, zero))
    qx = jnp.concatenate(parts, axis=0)
    rows = 6 * tq

    def chunk(i):
        start = pl.multiple_of(i * tk, tk)
        return k_ref[0, pl.ds(start, tk), :], v_ref[0, pl.ds(start, tk), :]

    def bounded_body(i, carry):
        lsum, acc = carry
        kc, vc = chunk(i)
        p = jnp.exp2(_dot_nt(qx, kc))
        for c in range(tk // LANES):
            lsum = lsum + p[:, c * LANES:(c + 1) * LANES]
        return lsum, acc + _dot(p.astype(BF16), vc)

    def online_body(i, carry):
        m, l, acc = carry
        kc, vc = chunk(i)
        s = _dot_nt(qx, kc)
        m_new = jnp.maximum(m, jnp.max(s, axis=1, keepdims=True))
        p = jnp.exp2(s - m_new)
        a = jnp.exp2(m - m_new)
        l = a * l + jnp.sum(p, axis=1, keepdims=True)
        acc = a * acc + _dot(p.astype(BF16), vc)
        return m_new, l, acc

    acc0 = jnp.zeros((rows, LANES), F32)
    if bounded:
        lsum, acc = lax.fori_loop(0, n_k, bounded_body, (acc0, acc0))
        l = jnp.sum(lsum, axis=1, keepdims=True)
    else:
        m0 = jnp.full((rows, 1), MASK_VALUE, F32)
        l0 = jnp.zeros((rows, 1), F32)
        _, l, acc = lax.fori_loop(0, n_k, online_body, (m0, l0, acc0))
    out = acc / l
    for j in range(3):
        o_ref[:, j * LANES:(j + 1) * LANES] = jnp.where(
            low, out[j * tq:(j + 1) * tq], out[(3 + j) * tq:(4 + j) * tq]).astype(BF16)


SCORE_BOUND = 60.0


def _gqa(q2d, k3d, v3d, gain_q, gain_k, seq_len, tq, tk):
    bound = HEAD_DIM ** 0.5 * jnp.max(jnp.abs(gain_q)) * jnp.max(jnp.abs(gain_k))
    return lax.cond(bound <= SCORE_BOUND,
                    lambda: _gqa_call(q2d, k3d, v3d, seq_len, tq, tk, True),
                    lambda: _gqa_call(q2d, k3d, v3d, seq_len, tq, tk, False))


def _gqa_call(q2d, k3d, v3d, seq_len, tq, tk, bounded):
    n = q2d.shape[0]
    lk = k3d.shape[1]
    tiles = seq_len // tq
    return pl.pallas_call(
        functools.partial(_gqa_kernel, tq=tq, tk=tk, bounded=bounded),
        out_shape=jax.ShapeDtypeStruct((n, GQA_WIDTH), BF16),
        grid=(n // tq,),
        in_specs=[pl.BlockSpec((tq, GQA_WIDTH), lambda i: (i, 0)),
                  pl.BlockSpec((1, lk, GQA_KV_WIDTH), lambda i: (i // tiles, 0, 0)),
                  pl.BlockSpec((1, lk, GQA_KV_WIDTH), lambda i: (i // tiles, 0, 0))],
        out_specs=pl.BlockSpec((tq, GQA_WIDTH), lambda i: (i, 0)),
        compiler_params=_params(1),
        name="gqa",
    )(q2d, k3d, v3d)


def _pair_queries(qb, low):
    zero = jnp.zeros_like(qb)
    return jnp.concatenate([jnp.where(low, qb, zero), jnp.where(low, zero, qb)], axis=0)


def _ctx_mha_kernel(q_ref, k_ref, v_ref, o_ref):
    rows = q_ref.shape[0]
    lane = lax.broadcasted_iota(jnp.int32, (rows, LANES), 1)
    low = lane < HEAD_DIM
    for j in range(NA_WIDTH // LANES):
        cols = slice(j * LANES, (j + 1) * LANES)
        qx = _pair_queries(q_ref[:, cols], low)
        s = _dot_nt(qx, k_ref[:, cols])
        p = jnp.exp2(s - jnp.max(s, axis=1, keepdims=True))
        l = jnp.sum(p, axis=1, keepdims=True)
        o = _dot(p.astype(BF16), v_ref[:, cols]) / l
        o_ref[:, cols] = jnp.where(low, o[:rows], o[rows:]).astype(BF16)


def _ctx_mha(na_c, ctx_len):
    n = na_c.shape[0]
    return pl.pallas_call(
        _ctx_mha_kernel,
        out_shape=jax.ShapeDtypeStruct((n, NA_WIDTH), BF16),
        grid=(n // ctx_len,),
        in_specs=[pl.BlockSpec((ctx_len, NA_WIDTH), lambda b: (b, 0)),
                  pl.BlockSpec((ctx_len, NA_WIDTH), lambda b: (b, 1)),
                  pl.BlockSpec((ctx_len, NA_WIDTH), lambda b: (b, 2))],
        out_specs=pl.BlockSpec((ctx_len, NA_WIDTH), lambda b: (b, 0)),
        compiler_params=_params(1),
        name="ctx_mha",
    )(na_c, na_c, na_c)


NA_BLOCK_ROWS = 8
NA_BLOCK = NA_BLOCK_ROWS * GRID_W
NA_BAND = NA_ROWS * GRID_W


def _na_kernel(q_ref, kp_ref, kc_ref, kn_ref, vp_ref, vc_ref, vn_ref, kx_ref, vx_ref, bias_ref,
               o_ref, kband, vband, *, n_rows):
    i = pl.program_id(1)
    kband[0:NA_BLOCK] = kp_ref[...]
    kband[NA_BLOCK:2 * NA_BLOCK] = kc_ref[...]
    kband[2 * NA_BLOCK:3 * NA_BLOCK] = kn_ref[...]
    vband[0:NA_BLOCK] = vp_ref[...]
    vband[NA_BLOCK:2 * NA_BLOCK] = vc_ref[...]
    vband[2 * NA_BLOCK:3 * NA_BLOCK] = vn_ref[...]
    lane = lax.broadcasted_iota(jnp.int32, (GRID_W, LANES), 1)
    low = lane < HEAD_DIM

    for j in range(NA_BLOCK_ROWS):
        r = i * NA_BLOCK_ROWS + j
        r_start = jnp.clip(r - NA_ROWS // 2, 0, n_rows - NA_ROWS)
        shift = r - r_start
        off = pl.multiple_of((r_start - (i - 1) * NA_BLOCK_ROWS) * GRID_W, GRID_W)
        qrows = slice(j * GRID_W, (j + 1) * GRID_W)
        for p in range(NA_WIDTH // LANES):
            cols = slice(p * LANES, (p + 1) * LANES)
            qx = _pair_queries(q_ref[qrows, cols], low)
            s_nb = _dot_nt(qx, kband[pl.ds(off, NA_BAND), cols]) + bias_ref[p, shift]
            s_cx = _dot_nt(qx, kx_ref[:, cols])
            m = jnp.maximum(jnp.max(s_nb, axis=1, keepdims=True), jnp.max(s_cx, axis=1, keepdims=True))
            p_nb = jnp.exp2(s_nb - m)
            p_cx = jnp.exp2(s_cx - m)
            l = jnp.sum(p_nb, axis=1, keepdims=True) + jnp.sum(p_cx, axis=1, keepdims=True)
            o = (_dot(p_nb.astype(BF16), vband[pl.ds(off, NA_BAND), cols])
                 + _dot(p_cx.astype(BF16), vx_ref[:, cols])) / l
            o_ref[qrows, cols] = jnp.where(low, o[:GRID_W], o[GRID_W:]).astype(BF16)


def _na_bias_table(rpb):
    depth = rpb.shape[0]
    col = jnp.arange(GRID_W)
    c_start = jnp.clip(col - NA_COLS // 2, 0, GRID_W - NA_COLS)
    in_win = (col[None, :] >= c_start[:, None]) & (col[None, :] < c_start[:, None] + NA_COLS)
    dc = col[None, :] - col[:, None] + NA_COLS - 1
    sel_c = ((dc[:, :, None] == jnp.arange(2 * NA_COLS - 1)) & in_win[:, :, None]).astype(F32)
    dr = jnp.arange(NA_ROWS)[None, :] - jnp.arange(NA_ROWS)[:, None] + NA_ROWS - 1
    sel_r = (dr[:, :, None] == jnp.arange(2 * NA_ROWS - 1)).astype(F32)
    pairs = rpb.astype(F32).reshape(depth, NA_HEADS // 2, 2, 2 * NA_ROWS - 1, 2 * NA_COLS - 1) * LOG2_E
    t = jnp.einsum("sir,lpurd,ckd->lpsucik", sel_r, pairs, sel_c, precision=lax.Precision.HIGHEST)
    t = jnp.where(in_win[:, None, :], t, MASK_VALUE)
    return t.reshape(depth, NA_HEADS // 2, NA_ROWS, 2 * GRID_W, NA_BAND)


def _na(na, na_c, bias, batch, seq_len, ctx_len):
    n = na.shape[0]
    n_rows = seq_len // GRID_W
    blocks = seq_len // NA_BLOCK
    tok = lambda col: (lambda b, i: (b * blocks + i, col))
    prev = lambda col: (lambda b, i: (b * blocks + jnp.maximum(i - 1, 0), col))
    nxt = lambda col: (lambda b, i: (b * blocks + jnp.minimum(i + 1, blocks - 1), col))
    blk = lambda f: pl.BlockSpec((NA_BLOCK, NA_WIDTH), f)
    return pl.pallas_call(
        functools.partial(_na_kernel, n_rows=n_rows),
        out_shape=jax.ShapeDtypeStruct((n, NA_WIDTH), BF16),
        grid=(batch, blocks),
        in_specs=[blk(tok(0)),
                  blk(prev(1)), blk(tok(1)), blk(nxt(1)),
                  blk(prev(2)), blk(tok(2)), blk(nxt(2)),
                  pl.BlockSpec((ctx_len, NA_WIDTH), lambda b, i: (b, 1)),
                  pl.BlockSpec((ctx_len, NA_WIDTH), lambda b, i: (b, 2)),
                  _const_spec(bias.shape)],
        out_specs=blk(tok(0)),
        scratch_shapes=[pltpu.VMEM((3 * NA_BLOCK, NA_WIDTH), BF16),
                        pltpu.VMEM((3 * NA_BLOCK, NA_WIDTH), BF16)],
        compiler_params=_params(2),
        name="natten",
    )(na, na, na, na, na, na, na, na_c, na_c, bias)


def _outproj_kernel(pp_ref, pm_ref, pn_ref, yna_ref, ygq_ref, x_ref, g1_ref, wout_ref, pw_ref, ps_ref,
                    lng_ref, lnb_ref, o_ref, *, seq_len, tm):
    tiles = seq_len // tm
    ti = pl.program_id(0) % tiles
    prev = jnp.where(ti == 0, 0.0, pp_ref[...])
    nxt = jnp.where(ti == tiles - 1, 0.0, pn_ref[...])
    u = jnp.concatenate([prev, pm_ref[...], nxt], axis=0)
    n = tm + 2 * POOL_HALO
    s2 = u + pltpu.roll(u, 1, 0)
    s4 = pltpu.roll(s2, 1, 0) + pltpu.roll(s2, n - 1, 0)
    s8 = pltpu.roll(s4, 2, 0) + pltpu.roll(s4, n - 2, 0)
    s16 = pltpu.roll(s8, 4, 0) + pltpu.roll(s8, n - 4, 0)
    mid = slice(POOL_HALO, POOL_HALO + tm)
    pos = ti * tm + lax.broadcasted_iota(jnp.int32, (tm, POOL_WIDTH), 0)
    grp = jnp.right_shift(lax.broadcasted_iota(jnp.int32, (tm, POOL_WIDTH), 1), 6)
    half = jnp.left_shift(1, grp)
    cnt = jnp.minimum(pos + half - 1, seq_len - 1) - jnp.maximum(pos - half, 0) + 1
    wsum = jnp.where(grp == 0, s2[mid], jnp.where(grp == 1, s4[mid], jnp.where(grp == 2, s8[mid], s16[mid])))
    pooled = wsum / cnt.astype(F32) - u[mid]
    y_pool = _dot(pooled.astype(BF16), pw_ref[...]) * ps_ref[...]
    mix = (_dot(y_pool.astype(BF16), wout_ref[0:POOL_WIDTH])
           + _dot(yna_ref[...], wout_ref[POOL_WIDTH:POOL_WIDTH + NA_WIDTH])
           + _dot(ygq_ref[...], wout_ref[POOL_WIDTH + NA_WIDTH:D_MODEL]))
    y = DEEPNORM_ALPHA * x_ref[...] + g1_ref[0] * mix
    o_ref[...] = _layer_norm(y, lng_ref[...], lnb_ref[...])


def _outproj(pool_in, y_na, y_gqa, x2d, mod, mod_row, w_out, pool_w_bd, pool_scale, ln_g, ln_b, seq_len, tm):
    n = x2d.shape[0]
    tiles = seq_len // tm
    hb = tm // POOL_HALO
    last_hb = n // POOL_HALO - 1
    tok = lambda i: (i, 0)
    return pl.pallas_call(
        functools.partial(_outproj_kernel, seq_len=seq_len, tm=tm),
        out_shape=jax.ShapeDtypeStruct((n, D_MODEL), F32),
        grid=(n // tm,),
        in_specs=[pl.BlockSpec((POOL_HALO, POOL_WIDTH), lambda i: (jnp.maximum(i * hb - 1, 0), 0)),
                  pl.BlockSpec((tm, POOL_WIDTH), tok),
                  pl.BlockSpec((POOL_HALO, POOL_WIDTH), lambda i: (jnp.minimum((i + 1) * hb, last_hb), 0)),
                  pl.BlockSpec((tm, NA_WIDTH), tok),
                  pl.BlockSpec((tm, GQA_WIDTH), tok),
                  pl.BlockSpec((tm, D_MODEL), tok),
                  pl.BlockSpec((1, 1, D_MODEL), lambda i: (mod_row(i // tiles), 0, 2)),
                  _const_spec((D_MODEL, D_MODEL)),
                  _const_spec((POOL_WIDTH, POOL_WIDTH)),
                  _const_spec((1, POOL_WIDTH)),
                  _const_spec((1, D_MODEL)),
                  _const_spec((1, D_MODEL))],
        out_specs=pl.BlockSpec((tm, D_MODEL), tok),
        compiler_params=_params(1),
        name="outproj",
    )(pool_in, pool_in, pool_in, y_na, y_gqa, x2d, mod, w_out, pool_w_bd, pool_scale, ln_g, ln_b)


FFN_HALO = 8


def _ffn_kernel(xp_ref, xm_ref, xn_ref, sh_ref, sc_ref, g2_ref, wup_ref, cw_ref, cb_ref, wdn_ref,
                lng_ref, lnb_ref, o_ref, *, seq_len, tm):
    tiles = seq_len // tm
    ti = pl.program_id(0) % tiles
    scale = 1.0 + sc_ref[0]
    shift = sh_ref[0]
    xm = xm_ref[...]
    hp = jnp.where(ti == 0, 0.0, xp_ref[...] * scale + shift)
    hn = jnp.where(ti == tiles - 1, 0.0, xn_ref[...] * scale + shift)
    h = jnp.concatenate([hp, xm * scale + shift, hn], axis=0).astype(BF16)
    n = tm + 2 * FFN_HALO
    mid = slice(FFN_HALO, FFN_HALO + tm)

    def conv(u, c0):
        cw = cw_ref[:, c0:c0 + FF_CHUNK]
        y = (pltpu.roll(u, 1, 0) * cw[0:1] + cb_ref[:, c0:c0 + FF_CHUNK]
             + u * cw[1:2] + pltpu.roll(u, n - 1, 0) * cw[2:3])
        return y[mid]

    acc = jnp.zeros((tm, D_MODEL), F32)
    for ch in range(D_FF // FF_CHUNK):
        a0 = ch * FF_CHUNK
        g0 = D_FF + a0
        a = conv(_dot(h, wup_ref[:, a0:a0 + FF_CHUNK]), a0)
        g = conv(_dot(h, wup_ref[:, g0:g0 + FF_CHUNK]), g0)
        act = (a * (g * jax.nn.sigmoid(g))).astype(BF16)
        acc = acc + _dot(act, wdn_ref[a0:a0 + FF_CHUNK])
    y = DEEPNORM_ALPHA * xm + g2_ref[0] * acc
    o_ref[...] = _layer_norm(y, lng_ref[...], lnb_ref[...])


def _ffn(x2d, mod, mod_row, w_up, conv_w, conv_b, w_down, ln_g, ln_b, seq_len, tm):
    n = x2d.shape[0]
    tiles = seq_len // tm
    hb = tm // FFN_HALO
    last_hb = n // FFN_HALO - 1
    tok = lambda i: (i, 0)
    mod_spec = lambda k: pl.BlockSpec((1, 1, D_MODEL), lambda i: (mod_row(i // tiles), 0, k))
    return pl.pallas_call(
        functools.partial(_ffn_kernel, seq_len=seq_len, tm=tm),
        out_shape=jax.ShapeDtypeStruct((n, D_MODEL), F32),
        grid=(n // tm,),
        in_specs=[pl.BlockSpec((FFN_HALO, D_MODEL), lambda i: (jnp.maximum(i * hb - 1, 0), 0)),
                  pl.BlockSpec((tm, D_MODEL), tok),
                  pl.BlockSpec((FFN_HALO, D_MODEL), lambda i: (jnp.minimum((i + 1) * hb, last_hb), 0)),
                  mod_spec(3), mod_spec(4), mod_spec(5),
                  _const_spec((D_MODEL, 2 * D_FF)),
                  _const_spec((3, 2 * D_FF)),
                  _const_spec((1, 2 * D_FF)),
                  _const_spec((D_FF, D_MODEL)),
                  _const_spec((1, D_MODEL)),
                  _const_spec((1, D_MODEL))],
        out_specs=pl.BlockSpec((tm, D_MODEL), tok),
        compiler_params=_params(1),
        name="convffn",
    )(x2d, x2d, x2d, mod, mod, mod, w_up, conv_w, conv_b, w_down, ln_g, ln_b)


def _rope_tables(seq_len):
    t = jnp.arange(seq_len, dtype=jnp.int32)
    half = HEAD_DIM // 2
    inv = ROPE_THETA ** (-jnp.arange(0, half, 2, dtype=F32) / half)
    ang_r = (t // GRID_W).astype(F32)[:, None] * inv
    ang_c = (t % GRID_W).astype(F32)[:, None] * inv
    cos = jnp.concatenate([jnp.cos(ang_r)] * 2 + [jnp.cos(ang_c)] * 2, axis=-1)
    sin = jnp.concatenate([-jnp.sin(ang_r), jnp.sin(ang_r), -jnp.sin(ang_c), jnp.sin(ang_c)], axis=-1)
    return jnp.tile(cos, (1, 2)), jnp.tile(sin, (1, 2))


def _pair_head_order():
    heads = jnp.array([0, 3, 1, 4, 2, 5])
    return (heads[:, None] * HEAD_DIM + jnp.arange(HEAD_DIM)[None, :]).reshape(-1)


def kernel(x, c, ctx, c_ctx, w_mod, b_mod, w_in, pool_w, pool_scale, na_rpb, q_norm, k_norm, w_out,
           ln1_g, ln1_b, w_up, conv_w, conv_b, w_down, ln2_g, ln2_b):
    batch, seq_len, d = x.shape
    ctx_len = ctx.shape[1]
    depth = w_mod.shape[0]
    assert d == D_MODEL and seq_len % NA_BLOCK == 0 and seq_len // GRID_W >= NA_ROWS
    tm = 512
    tm_c = ctx_len
    tq = 256
    tk = 768 if (seq_len + ctx_len) % 768 == 0 else 256
    assert (seq_len + ctx_len) % tk == 0 and seq_len % tq == 0 and ctx_len % 256 == 0

    mod_rows = -(-(batch + 1) // 8) * 8
    c_rows = jnp.zeros((mod_rows, d), F32).at[:batch].set(c).at[batch].set(c_ctx)
    mods = _modulation(c_rows, w_mod, b_mod)
    lat_row = lambda b: b
    ctx_row = lambda b: batch

    order = _pair_head_order()
    q_cols = _C_GQ + order
    w_in_p = jnp.concatenate([w_in[:, :, :_C_GQ], w_in[:, :, q_cols], w_in[:, :, _C_GK:]], axis=-1).astype(BF16)
    o_rows = POOL_WIDTH + NA_WIDTH + order
    w_out_p = jnp.concatenate([w_out[:, :POOL_WIDTH + NA_WIDTH], w_out[:, o_rows]], axis=1).astype(BF16)
    w_up_b = w_up.astype(BF16)
    w_down_b = w_down.astype(BF16)
    eye = jnp.eye(pool_w.shape[1], dtype=F32)
    pool_w_bd = jnp.einsum("gh,lgcd->lgchd", eye, pool_w).reshape(depth, POOL_WIDTH, POOL_WIDTH).astype(BF16)

    na_bias = _na_bias_table(na_rpb)
    cos, sin = _rope_tables(seq_len)
    cos_c = jnp.ones((ctx_len, LANES), F32)
    sin_c = jnp.zeros((ctx_len, LANES), F32)

    xl = x.reshape(batch * seq_len, d)
    xc = ctx.reshape(batch * ctx_len, d)
    for l in range(depth):
        mod = mods[l].reshape(mod_rows, 1, 6 * d)
        qg = jnp.tile(q_norm[l], 2).reshape(1, LANES)
        kg = jnp.tile(k_norm[l], 2).reshape(1, LANES)
        ps = pool_scale[l].reshape(1, POOL_WIDTH)
        g1, b1 = ln1_g[l].reshape(1, d), ln1_b[l].reshape(1, d)
        g2, b2 = ln2_g[l].reshape(1, d), ln2_b[l].reshape(1, d)
        cb = conv_b[l].reshape(1, 2 * D_FF)

        p_c, na_c, q_c, k_c, v_c = _inproj(xc, mod, ctx_row, w_in_p[l], cos_c, sin_c, qg, kg, ctx_len, tm_c)
        p_l, na_l, q_l, k_l, v_l = _inproj(xl, mod, lat_row, w_in_p[l], cos, sin, qg, kg, seq_len, tm)

        k_c3 = k_c.reshape(batch, ctx_len, GQA_KV_WIDTH)
        v_c3 = v_c.reshape(batch, ctx_len, GQA_KV_WIDTH)
        k_all = jnp.concatenate([k_c3, k_l.reshape(batch, seq_len, GQA_KV_WIDTH)], axis=1)
        v_all = jnp.concatenate([v_c3, v_l.reshape(batch, seq_len, GQA_KV_WIDTH)], axis=1)
        y_gqa = _gqa(q_l, k_all, v_all, q_norm[l], k_norm[l], seq_len, tq, tk)
        y_na = _na(na_l, na_c, na_bias[l], batch, seq_len, ctx_len)
        x1 = _outproj(p_l, y_na, y_gqa, xl, mod, lat_row, w_out_p[l], pool_w_bd[l], ps, g1, b1, seq_len, tm)
        xl = _ffn(x1, mod, lat_row, w_up_b[l], conv_w[l], cb, w_down_b[l], g2, b2, seq_len, tm)

        if l < depth - 1:
            y_na_c = _ctx_mha(na_c, ctx_len)
            y_gqa_c = _gqa(q_c, k_c3, v_c3, q_norm[l], k_norm[l], ctx_len, ctx_len, ctx_len)
            xc1 = _outproj(p_c, y_na_c, y_gqa_c, xc, mod, ctx_row, w_out_p[l], pool_w_bd[l], ps, g1, b1,
                           ctx_len, tm_c)
            xc = _ffn(xc1, mod, ctx_row, w_up_b[l], conv_w[l], cb, w_down_b[l], g2, b2, ctx_len, tm_c)
    return xl.reshape(batch, seq_len, d)
```

```python
import functools

import jax
import jax.numpy as jnp
from jax import lax
from jax.experimental import pallas as pl
from jax.experimental.pallas import tpu as pltpu

F32 = jnp.float32
BF16 = jnp.bfloat16

D_MODEL = 1024
GRID_W = 64
HEAD_DIM = 64
LANES = 128
POOL_WIDTH = 256
POOL_WINDOWS = (2, 4, 8, 16)
POOL_HALO = 8
NA_WIDTH = 384
NA_HEADS = 6
NA_ROWS = 8
NA_COLS = 16
GQA_WIDTH = 384
GQA_KV_WIDTH = 128
D_FF = 2816
FF_CHUNK = 2816
ROPE_THETA = 10000.0
DEEPNORM_ALPHA = 8.0 ** 0.25
LN_EPS = 1e-6
MASK_VALUE = -1e30
LOG2_E = 1.4426950408889634
VMEM_LIMIT = 56 * 1024 * 1024

_C_POOL = 0
_C_NA = POOL_WIDTH
_C_GQ = _C_NA + 3 * NA_WIDTH
_C_GK = _C_GQ + GQA_WIDTH
_C_GV = _C_GK + GQA_KV_WIDTH
IN_WIDTH = _C_GV + GQA_KV_WIDTH


def _params(n_axes):
    return pltpu.CompilerParams(dimension_semantics=("parallel",) * n_axes,
                                vmem_limit_bytes=VMEM_LIMIT)


def _const_spec(shape):
    return pl.BlockSpec(shape, lambda *_: (0,) * len(shape), pipeline_mode=pl.Buffered(1))


def _dot(a, b):
    return jnp.dot(a, b, preferred_element_type=F32)


def _dot_nt(a, b):
    return lax.dot_general(a, b, (((1,), (1,)), ((), ())), preferred_element_type=F32)


def _layer_norm(y, g, b):
    mu = jnp.mean(y, axis=-1, keepdims=True)
    d = y - mu
    var = jnp.mean(d * d, axis=-1, keepdims=True)
    return d * lax.rsqrt(var + LN_EPS) * g + b


def _mod_kernel(c_ref, w_ref, b_ref, o_ref):
    c = c_ref[...]
    act = c * jax.nn.sigmoid(c)
    o_ref[0] = _dot(act, w_ref[0]) + b_ref[0]


def _modulation(c_rows, w_mod, b_mod):
    depth, d, n = w_mod.shape
    rows = c_rows.shape[0]
    bn = 1536
    return pl.pallas_call(
        _mod_kernel,
        out_shape=jax.ShapeDtypeStruct((depth, rows, n), F32),
        grid=(depth, n // bn),
        in_specs=[pl.BlockSpec((rows, d), lambda l, j: (0, 0)),
                  pl.BlockSpec((1, d, bn), lambda l, j: (l, 0, j)),
                  pl.BlockSpec((1, 1, bn), lambda l, j: (l, 0, j))],
        out_specs=pl.BlockSpec((1, rows, bn), lambda l, j: (l, 0, j)),
        compiler_params=_params(2),
        name="modulation",
    )(c_rows, w_mod, b_mod.reshape(depth, 1, n))


def _swap16(v):
    lane = lax.broadcasted_iota(jnp.int32, v.shape, 1)
    even = jnp.bitwise_and(lane, 16) == 0
    return jnp.where(even, pltpu.roll(v, LANES - 16, 1), pltpu.roll(v, 16, 1))


def _head_sumsq(z):
    width = z.shape[1]
    r = jnp.right_shift(lax.broadcasted_iota(jnp.int32, (width, width), 0), 6)
    c = jnp.right_shift(lax.broadcasted_iota(jnp.int32, (width, width), 1), 6)
    return _dot((z * z).astype(BF16), (r == c).astype(BF16))


def _norm_rope(z, ss, gain, cos, sin):
    zn = z * lax.rsqrt(ss * (1.0 / HEAD_DIM) + LN_EPS) * gain
    return zn * cos + _swap16(zn) * sin


def _inproj_kernel(x_ref, sh_ref, sc_ref, w_ref, cos_ref, sin_ref, qg_ref, kg_ref,
                   pool_ref, na_ref, q_ref, k_ref, v_ref):
    h = (x_ref[...] * (1.0 + sc_ref[0]) + sh_ref[0]).astype(BF16)
    pool_ref[...] = _dot(h, w_ref[:, _C_POOL:_C_NA])
    na_ref[:, 0:NA_WIDTH] = (_dot(h, w_ref[:, _C_NA:_C_NA + NA_WIDTH])
                             * (HEAD_DIM ** -0.5 * LOG2_E)).astype(BF16)
    na_ref[:, NA_WIDTH:3 * NA_WIDTH] = _dot(h, w_ref[:, _C_NA + NA_WIDTH:_C_GQ]).astype(BF16)
    cos = cos_ref[...]
    sin = sin_ref[...]
    zqk = _dot(h, w_ref[:, _C_GQ:_C_GV])
    ss = jnp.concatenate([_head_sumsq(zqk[:, 0:2 * LANES]), _head_sumsq(zqk[:, 2 * LANES:4 * LANES])], axis=1)
    for j in range(GQA_WIDTH // LANES):
        cols = slice(j * LANES, (j + 1) * LANES)
        blk = _norm_rope(zqk[:, cols], ss[:, cols], qg_ref[...], cos, sin)
        q_ref[:, cols] = (blk * (HEAD_DIM ** -0.5 * LOG2_E)).astype(BF16)
    cols = slice(GQA_WIDTH, GQA_WIDTH + GQA_KV_WIDTH)
    k_ref[...] = _norm_rope(zqk[:, cols], ss[:, cols], kg_ref[...], cos, sin).astype(BF16)
    v_ref[...] = _dot(h, w_ref[:, _C_GV:IN_WIDTH]).astype(BF16)


def _inproj(x2d, mod, mod_row, w_in, cos, sin, qg, kg, seq_len, tm):
    n = x2d.shape[0]
    tiles = seq_len // tm
    tok = lambda i: (i, 0)
    outs = (jax.ShapeDtypeStruct((n, POOL_WIDTH), F32),
            jax.ShapeDtypeStruct((n, 3 * NA_WIDTH), BF16),
            jax.ShapeDtypeStruct((n, GQA_WIDTH), BF16),
            jax.ShapeDtypeStruct((n, GQA_KV_WIDTH), BF16),
            jax.ShapeDtypeStruct((n, GQA_KV_WIDTH), BF16))
    return pl.pallas_call(
        _inproj_kernel,
        out_shape=outs,
        grid=(n // tm,),
        in_specs=[pl.BlockSpec((tm, D_MODEL), tok),
                  pl.BlockSpec((1, 1, D_MODEL), lambda i: (mod_row(i // tiles), 0, 0)),
                  pl.BlockSpec((1, 1, D_MODEL), lambda i: (mod_row(i // tiles), 0, 1)),
                  _const_spec((D_MODEL, IN_WIDTH)),
                  pl.BlockSpec((tm, LANES), lambda i: (i % tiles, 0)),
                  pl.BlockSpec((tm, LANES), lambda i: (i % tiles, 0)),
                  _const_spec((1, LANES)),
                  _const_spec((1, LANES))],
        out_specs=[pl.BlockSpec((tm, POOL_WIDTH), tok),
                   pl.BlockSpec((tm, 3 * NA_WIDTH), tok),
                   pl.BlockSpec((tm, GQA_WIDTH), tok),
                   pl.BlockSpec((tm, GQA_KV_WIDTH), tok),
                   pl.BlockSpec((tm, GQA_KV_WIDTH), tok)],
        compiler_params=_params(1),
        name="inproj",
    )(x2d, mod, mod, w_in, cos, sin, qg, kg)


def _gqa_kernel(q_ref, k_ref, v_ref, o_ref, *, tq, tk, bounded):
    n_k = k_ref.shape[1] // tk
    lane = lax.broadcasted_iota(jnp.int32, (tq, LANES), 1)
    low = lane < HEAD_DIM
    zero = jnp.zeros((tq, LANES), BF16)
    parts = []
    for grp in range(2):
        for j in range(3):
            blk = q_ref[:, j * LANES:(j + 1) * LANES]
            parts.append(jnp.where(low if grp == 0 else jnp.logical_not(low), blk, zero))
    qx = jnp.concatenate(parts, axis=0)
    rows = 6 * tq

    def chunk(i):
        start = i * tk if isinstance(i, int) else pl.multiple_of(i * tk, tk)
        return k_ref[0, pl.ds(start, tk), :], v_ref[0, pl.ds(start, tk), :]

    def bounded_body(i, carry):
        lsum, acc = carry
        kc, vc = chunk(i)
        p = jnp.exp2(_dot_nt(qx, kc))
        for c in range(tk // LANES):
            lsum = lsum + p[:, c * LANES:(c + 1) * LANES]
        return lsum, acc + _dot(p.astype(BF16), vc)

    def online_body(i, carry):
        m, l, acc = carry
        kc, vc = chunk(i)
        s = _dot_nt(qx, kc)
        m_new = jnp.maximum(m, jnp.max(s, axis=1, keepdims=True))
        p = jnp.exp2(s - m_new)
        a = jnp.exp2(m - m_new)
        l = a * l + jnp.sum(p, axis=1, keepdims=True)
        acc = a * acc + _dot(p.astype(BF16), vc)
        return m_new, l, acc

    acc0 = jnp.zeros((rows, LANES), F32)
    if bounded:
        lsum, acc = lax.fori_loop(0, n_k, bounded_body, (acc0, acc0))
        l = jnp.sum(lsum, axis=1, keepdims=True)
    else:
        m0 = jnp.full((rows, 1), MASK_VALUE, F32)
        l0 = jnp.zeros((rows, 1), F32)
        _, l, acc = lax.fori_loop(0, n_k, online_body, (m0, l0, acc0))
    out = acc / l
    for j in range(3):
        o_ref[:, j * LANES:(j + 1) * LANES] = jnp.where(
            low, out[j * tq:(j + 1) * tq], out[(3 + j) * tq:(4 + j) * tq]).astype(BF16)


SCORE_BOUND = 60.0


def _gqa(q2d, k3d, v3d, gain_q, gain_k, seq_len, tq, tk):
    bound = HEAD_DIM ** 0.5 * jnp.max(jnp.abs(gain_q)) * jnp.max(jnp.abs(gain_k))
    return lax.cond(bound <= SCORE_BOUND,
                    lambda: _gqa_call(q2d, k3d, v3d, seq_len, tq, tk, True),
                    lambda: _gqa_call(q2d, k3d, v3d, seq_len, tq, tk, False))


def _gqa_call(q2d, k3d, v3d, seq_len, tq, tk, bounded):
    n = q2d.shape[0]
    lk = k3d.shape[1]
    tiles = seq_len // tq
    return pl.pallas_call(
        functools.partial(_gqa_kernel, tq=tq, tk=tk, bounded=bounded),
        out_shape=jax.ShapeDtypeStruct((n, GQA_WIDTH), BF16),
        grid=(n // tq,),
        in_specs=[pl.BlockSpec((tq, GQA_WIDTH), lambda i: (i, 0)),
                  pl.BlockSpec((1, lk, GQA_KV_WIDTH), lambda i: (i // tiles, 0, 0)),
                  pl.BlockSpec((1, lk, GQA_KV_WIDTH), lambda i: (i // tiles, 0, 0))],
        out_specs=pl.BlockSpec((tq, GQA_WIDTH), lambda i: (i, 0)),
        compiler_params=_params(1),
        name="gqa",
    )(q2d, k3d, v3d)


def _pair_queries(qb, low):
    zero = jnp.zeros_like(qb)
    return jnp.concatenate([jnp.where(low, qb, zero), jnp.where(low, zero, qb)], axis=0)


def _ctx_mha_kernel(q_ref, k_ref, v_ref, o_ref):
    rows = q_ref.shape[0]
    lane = lax.broadcasted_iota(jnp.int32, (rows, LANES), 1)
    low = lane < HEAD_DIM
    for j in range(NA_WIDTH // LANES):
        cols = slice(j * LANES, (j + 1) * LANES)
        qx = _pair_queries(q_ref[:, cols], low)
        s = _dot_nt(qx, k_ref[:, cols])
        p = jnp.exp2(s - jnp.max(s, axis=1, keepdims=True))
        l = jnp.sum(p, axis=1, keepdims=True)
        o = _dot(p.astype(BF16), v_ref[:, cols]) / l
        o_ref[:, cols] = jnp.where(low, o[:rows], o[rows:]).astype(BF16)


def _ctx_mha(na_c, ctx_len):
    n = na_c.shape[0]
    return pl.pallas_call(
        _ctx_mha_kernel,
        out_shape=jax.ShapeDtypeStruct((n, NA_WIDTH), BF16),
        grid=(n // ctx_len,),
        in_specs=[pl.BlockSpec((ctx_len, NA_WIDTH), lambda b: (b, 0)),
                  pl.BlockSpec((ctx_len, NA_WIDTH), lambda b: (b, 1)),
                  pl.BlockSpec((ctx_len, NA_WIDTH), lambda b: (b, 2))],
        out_specs=pl.BlockSpec((ctx_len, NA_WIDTH), lambda b: (b, 0)),
        compiler_params=_params(1),
        name="ctx_mha",
    )(na_c, na_c, na_c)


NA_BLOCK_ROWS = 8
NA_BLOCK = NA_BLOCK_ROWS * GRID_W
NA_BAND = NA_ROWS * GRID_W


def _na_kernel(q_ref, kp_ref, kc_ref, kn_ref, vp_ref, vc_ref, vn_ref, kx_ref, vx_ref, bias_ref,
               o_ref, kband, vband, *, n_rows):
    i = pl.program_id(1)
    kband[0:NA_BLOCK] = kp_ref[...]
    kband[NA_BLOCK:2 * NA_BLOCK] = kc_ref[...]
    kband[2 * NA_BLOCK:3 * NA_BLOCK] = kn_ref[...]
    vband[0:NA_BLOCK] = vp_ref[...]
    vband[NA_BLOCK:2 * NA_BLOCK] = vc_ref[...]
    vband[2 * NA_BLOCK:3 * NA_BLOCK] = vn_ref[...]
    lane = lax.broadcasted_iota(jnp.int32, (GRID_W, LANES), 1)
    low = lane < HEAD_DIM

    for j in range(NA_BLOCK_ROWS):
        r = i * NA_BLOCK_ROWS + j
        r_start = jnp.clip(r - NA_ROWS // 2, 0, n_rows - NA_ROWS)
        shift = r - r_start
        off = pl.multiple_of((r_start - (i - 1) * NA_BLOCK_ROWS) * GRID_W, GRID_W)
        qrows = slice(j * GRID_W, (j + 1) * GRID_W)
        for p in range(NA_WIDTH // LANES):
            cols = slice(p * LANES, (p + 1) * LANES)
            qx = _pair_queries(q_ref[qrows, cols], low)
            s_nb = _dot_nt(qx, kband[pl.ds(off, NA_BAND), cols]) + bias_ref[p, shift]
            s_cx = _dot_nt(qx, kx_ref[:, cols])
            m = jnp.maximum(jnp.max(s_nb, axis=1, keepdims=True), jnp.max(s_cx, axis=1, keepdims=True))
            p_nb = jnp.exp2(s_nb - m)
            p_cx = jnp.exp2(s_cx - m)
            l = jnp.sum(p_nb, axis=1, keepdims=True) + jnp.sum(p_cx, axis=1, keepdims=True)
            o = (_dot(p_nb.astype(BF16), vband[pl.ds(off, NA_BAND), cols])
                 + _dot(p_cx.astype(BF16), vx_ref[:, cols])) / l
            o_ref[qrows, cols] = jnp.where(low, o[:GRID_W], o[GRID_W:]).astype(BF16)


def _na_bias_table(rpb):
    depth = rpb.shape[0]
    col = jnp.arange(GRID_W)
    c_start = jnp.clip(col - NA_COLS // 2, 0, GRID_W - NA_COLS)
    in_win = (col[None, :] >= c_start[:, None]) & (col[None, :] < c_start[:, None] + NA_COLS)
    dc = col[None, :] - col[:, None] + NA_COLS - 1
    sel_c = ((dc[:, :, None] == jnp.arange(2 * NA_COLS - 1)) & in_win[:, :, None]).astype(F32)
    dr = jnp.arange(NA_ROWS)[None, :] - jnp.arange(NA_ROWS)[:, None] + NA_ROWS - 1
    sel_r = (dr[:, :, None] == jnp.arange(2 * NA_ROWS - 1)).astype(F32)
    pairs = rpb.astype(F32).reshape(depth, NA_HEADS // 2, 2, 2 * NA_ROWS - 1, 2 * NA_COLS - 1) * LOG2_E
    t = jnp.einsum("sir,lpurd,ckd->lpsucik", sel_r, pairs, sel_c, precision=lax.Precision.HIGHEST)
    t = jnp.where(in_win[:, None, :], t, MASK_VALUE)
    return t.reshape(depth, NA_HEADS // 2, NA_ROWS, 2 * GRID_W, NA_BAND)


def _na(na, na_c, bias, batch, seq_len, ctx_len):
    n = na.shape[0]
    n_rows = seq_len // GRID_W
    blocks = seq_len // NA_BLOCK
    tok = lambda col: (lambda b, i: (b * blocks + i, col))
    prev = lambda col: (lambda b, i: (b * blocks + jnp.maximum(i - 1, 0), col))
    nxt = lambda col: (lambda b, i: (b * blocks + jnp.minimum(i + 1, blocks - 1), col))
    blk = lambda f: pl.BlockSpec((NA_BLOCK, NA_WIDTH), f)
    return pl.pallas_call(
        functools.partial(_na_kernel, n_rows=n_rows),
        out_shape=jax.ShapeDtypeStruct((n, NA_WIDTH), BF16),
        grid=(batch, blocks),
        in_specs=[blk(tok(0)),
                  blk(prev(1)), blk(tok(1)), blk(nxt(1)),
                  blk(prev(2)), blk(tok(2)), blk(nxt(2)),
                  pl.BlockSpec((ctx_len, NA_WIDTH), lambda b, i: (b, 1)),
                  pl.BlockSpec((ctx_len, NA_WIDTH), lambda b, i: (b, 2)),
                  _const_spec(bias.shape)],
        out_specs=blk(tok(0)),
        scratch_shapes=[pltpu.VMEM((3 * NA_BLOCK, NA_WIDTH), BF16),
                        pltpu.VMEM((3 * NA_BLOCK, NA_WIDTH), BF16)],
        compiler_params=_params(2),
        name="natten",
    )(na, na, na, na, na, na, na, na_c, na_c, bias)


def _outproj_kernel(pp_ref, pm_ref, pn_ref, yna_ref, ygq_ref, x_ref, g1_ref, wout_ref, pw_ref, ps_ref,
                    lng_ref, lnb_ref, o_ref, *, seq_len, tm):
    tiles = seq_len // tm
    ti = pl.program_id(0) % tiles
    prev = jnp.where(ti == 0, 0.0, pp_ref[...])
    nxt = jnp.where(ti == tiles - 1, 0.0, pn_ref[...])
    u = jnp.concatenate([prev, pm_ref[...], nxt], axis=0)
    n = tm + 2 * POOL_HALO
    s2 = u + pltpu.roll(u, 1, 0)
    s4 = pltpu.roll(s2, 1, 0) + pltpu.roll(s2, n - 1, 0)
    s8 = pltpu.roll(s4, 2, 0) + pltpu.roll(s4, n - 2, 0)
    s16 = pltpu.roll(s8, 4, 0) + pltpu.roll(s8, n - 4, 0)
    mid = slice(POOL_HALO, POOL_HALO + tm)
    pos = ti * tm + lax.broadcasted_iota(jnp.int32, (tm, POOL_WIDTH), 0)
    grp = jnp.right_shift(lax.broadcasted_iota(jnp.int32, (tm, POOL_WIDTH), 1), 6)
    half = jnp.left_shift(1, grp)
    cnt = jnp.minimum(pos + half - 1, seq_len - 1) - jnp.maximum(pos - half, 0) + 1
    wsum = jnp.where(grp == 0, s2[mid], jnp.where(grp == 1, s4[mid], jnp.where(grp == 2, s8[mid], s16[mid])))
    pooled = wsum / cnt.astype(F32) - u[mid]
    y_pool = _dot(pooled.astype(BF16), pw_ref[...]) * ps_ref[...]
    mix = (_dot(y_pool.astype(BF16), wout_ref[0:POOL_WIDTH])
           + _dot(yna_ref[...], wout_ref[POOL_WIDTH:POOL_WIDTH + NA_WIDTH])
           + _dot(ygq_ref[...], wout_ref[POOL_WIDTH + NA_WIDTH:D_MODEL]))
    y = DEEPNORM_ALPHA * x_ref[...] + g1_ref[0] * mix
    o_ref[...] = _layer_norm(y, lng_ref[...], lnb_ref[...])


def _outproj(pool_in, y_na, y_gqa, x2d, mod, mod_row, w_out, pool_w_bd, pool_scale, ln_g, ln_b, seq_len, tm):
    n = x2d.shape[0]
    tiles = seq_len // tm
    hb = tm // POOL_HALO
    last_hb = n // POOL_HALO - 1
    tok = lambda i: (i, 0)
    return pl.pallas_call(
        functools.partial(_outproj_kernel, seq_len=seq_len, tm=tm),
        out_shape=jax.ShapeDtypeStruct((n, D_MODEL), F32),
        grid=(n // tm,),
        in_specs=[pl.BlockSpec((POOL_HALO, POOL_WIDTH), lambda i: (jnp.maximum(i * hb - 1, 0), 0)),
                  pl.BlockSpec((tm, POOL_WIDTH), tok),
                  pl.BlockSpec((POOL_HALO, POOL_WIDTH), lambda i: (jnp.minimum((i + 1) * hb, last_hb), 0)),
                  pl.BlockSpec((tm, NA_WIDTH), tok),
                  pl.BlockSpec((tm, GQA_WIDTH), tok),
                  pl.BlockSpec((tm, D_MODEL), tok),
                  pl.BlockSpec((1, 1, D_MODEL), lambda i: (mod_row(i // tiles), 0, 2)),
                  _const_spec((D_MODEL, D_MODEL)),
                  _const_spec((POOL_WIDTH, POOL_WIDTH)),
                  _const_spec((1, POOL_WIDTH)),
                  _const_spec((1, D_MODEL)),
                  _const_spec((1, D_MODEL))],
        out_specs=pl.BlockSpec((tm, D_MODEL), tok),
        compiler_params=_params(1),
        name="outproj",
    )(pool_in, pool_in, pool_in, y_na, y_gqa, x2d, mod, w_out, pool_w_bd, pool_scale, ln_g, ln_b)


FFN_HALO = 8


def _ffn_kernel(xp_ref, xm_ref, xn_ref, sh_ref, sc_ref, g2_ref, wup_ref, cw_ref, cb_ref, wdn_ref,
                lng_ref, lnb_ref, o_ref, *, seq_len, tm):
    tiles = seq_len // tm
    ti = pl.program_id(0) % tiles
    scale = 1.0 + sc_ref[0]
    shift = sh_ref[0]
    xm = xm_ref[...]
    hp = jnp.where(ti == 0, 0.0, xp_ref[...] * scale + shift)
    hn = jnp.where(ti == tiles - 1, 0.0, xn_ref[...] * scale + shift)
    h = jnp.concatenate([hp, xm * scale + shift, hn], axis=0).astype(BF16)
    n = tm + 2 * FFN_HALO
    mid = slice(FFN_HALO, FFN_HALO + tm)

    def conv(u, c0):
        cw = cw_ref[:, c0:c0 + FF_CHUNK]
        y = (pltpu.roll(u, 1, 0) * cw[0:1] + cb_ref[:, c0:c0 + FF_CHUNK]
             + u * cw[1:2] + pltpu.roll(u, n - 1, 0) * cw[2:3])
        return y[mid]

    acc = jnp.zeros((tm, D_MODEL), F32)
    for ch in range(D_FF // FF_CHUNK):
        a0 = ch * FF_CHUNK
        g0 = D_FF + a0
        a = conv(_dot(h, wup_ref[:, a0:a0 + FF_CHUNK]), a0)
        g = conv(_dot(h, wup_ref[:, g0:g0 + FF_CHUNK]), g0)
        act = (a * (g * jax.nn.sigmoid(g))).astype(BF16)
        acc = acc + _dot(act, wdn_ref[a0:a0 + FF_CHUNK])
    y = DEEPNORM_ALPHA * xm + g2_ref[0] * acc
    o_ref[...] = _layer_norm(y, lng_ref[...], lnb_ref[...])


def _ffn(x2d, mod, mod_row, w_up, conv_w, conv_b, w_down, ln_g, ln_b, seq_len, tm):
    n = x2d.shape[0]
    tiles = seq_len // tm
    hb = tm // FFN_HALO
    last_hb = n // FFN_HALO - 1
    tok = lambda i: (i, 0)
    mod_spec = lambda k: pl.BlockSpec((1, 1, D_MODEL), lambda i: (mod_row(i // tiles), 0, k))
    return pl.pallas_call(
        functools.partial(_ffn_kernel, seq_len=seq_len, tm=tm),
        out_shape=jax.ShapeDtypeStruct((n, D_MODEL), F32),
        grid=(n // tm,),
        in_specs=[pl.BlockSpec((FFN_HALO, D_MODEL), lambda i: (jnp.maximum(i * hb - 1, 0), 0)),
                  pl.BlockSpec((tm, D_MODEL), tok),
                  pl.BlockSpec((FFN_HALO, D_MODEL), lambda i: (jnp.minimum((i + 1) * hb, last_hb), 0)),
                  mod_spec(3), mod_spec(4), mod_spec(5),
                  _const_spec((D_MODEL, 2 * D_FF)),
                  _const_spec((3, 2 * D_FF)),
                  _const_spec((1, 2 * D_FF)),
                  _const_spec((D_FF, D_MODEL)),
                  _const_spec((1, D_MODEL)),
                  _const_spec((1, D_MODEL))],
        out_specs=pl.BlockSpec((tm, D_MODEL), tok),
        compiler_params=_params(1),
        name="convffn",
    )(x2d, x2d, x2d, mod, mod, mod, w_up, conv_w, conv_b, w_down, ln_g, ln_b)


def _rope_tables(seq_len):
    t = jnp.arange(seq_len, dtype=jnp.int32)
    half = HEAD_DIM // 2
    inv = ROPE_THETA ** (-jnp.arange(0, half, 2, dtype=F32) / half)
    ang_r = (t // GRID_W).astype(F32)[:, None] * inv
    ang_c = (t % GRID_W).astype(F32)[:, None] * inv
    cos = jnp.concatenate([jnp.cos(ang_r)] * 2 + [jnp.cos(ang_c)] * 2, axis=-1)
    sin = jnp.concatenate([-jnp.sin(ang_r), jnp.sin(ang_r), -jnp.sin(ang_c), jnp.sin(ang_c)], axis=-1)
    return jnp.tile(cos, (1, 2)), jnp.tile(sin, (1, 2))


def _pair_head_order():
    heads = jnp.array([0, 3, 1, 4, 2, 5])
    return (heads[:, None] * HEAD_DIM + jnp.arange(HEAD_DIM)[None, :]).reshape(-1)


def kernel(x, c, ctx, c_ctx, w_mod, b_mod, w_in, pool_w, pool_scale, na_rpb, q_norm, k_norm, w_out,
           ln1_g, ln1_b, w_up, conv_w, conv_b, w_down, ln2_g, ln2_b):
    batch, seq_len, d = x.shape
    ctx_len = ctx.shape[1]
    depth = w_mod.shape[0]
    assert d == D_MODEL and seq_len % NA_BLOCK == 0 and seq_len // GRID_W >= NA_ROWS
    tm = 512
    tm_c = ctx_len
    tq = 256
    tk = next(t for t in (2816, 768, 256) if (seq_len + ctx_len) % t == 0)
    assert (seq_len + ctx_len) % tk == 0 and seq_len % tq == 0 and ctx_len % 256 == 0

    mod_rows = -(-(batch + 1) // 8) * 8
    c_rows = jnp.zeros((mod_rows, d), F32).at[:batch].set(c).at[batch].set(c_ctx)
    mods = _modulation(c_rows, w_mod, b_mod)
    lat_row = lambda b: b
    ctx_row = lambda b: batch

    order = _pair_head_order()
    q_cols = _C_GQ + order
    w_in_p = jnp.concatenate([w_in[:, :, :_C_GQ], w_in[:, :, q_cols], w_in[:, :, _C_GK:]], axis=-1).astype(BF16)
    o_rows = POOL_WIDTH + NA_WIDTH + order
    w_out_p = jnp.concatenate([w_out[:, :POOL_WIDTH + NA_WIDTH], w_out[:, o_rows]], axis=1).astype(BF16)
    w_up_b = w_up.astype(BF16)
    w_down_b = w_down.astype(BF16)
    eye = jnp.eye(pool_w.shape[1], dtype=F32)
    pool_w_bd = jnp.einsum("gh,lgcd->lgchd", eye, pool_w).reshape(depth, POOL_WIDTH, POOL_WIDTH).astype(BF16)

    na_bias = _na_bias_table(na_rpb)
    cos, sin = _rope_tables(seq_len)
    cos_c = jnp.ones((ctx_len, LANES), F32)
    sin_c = jnp.zeros((ctx_len, LANES), F32)

    xl = x.reshape(batch * seq_len, d)
    xc = ctx.reshape(batch * ctx_len, d)
    for l in range(depth):
        mod = mods[l].reshape(mod_rows, 1, 6 * d)
        qg = jnp.tile(q_norm[l], 2).reshape(1, LANES)
        kg = jnp.tile(k_norm[l], 2).reshape(1, LANES)
        ps = pool_scale[l].reshape(1, POOL_WIDTH)
        g1, b1 = ln1_g[l].reshape(1, d), ln1_b[l].reshape(1, d)
        g2, b2 = ln2_g[l].reshape(1, d), ln2_b[l].reshape(1, d)
        cb = conv_b[l].reshape(1, 2 * D_FF)

        p_c, na_c, q_c, k_c, v_c = _inproj(xc, mod, ctx_row, w_in_p[l], cos_c, sin_c, qg, kg, ctx_len, tm_c)
        p_l, na_l, q_l, k_l, v_l = _inproj(xl, mod, lat_row, w_in_p[l], cos, sin, qg, kg, seq_len, tm)

        k_c3 = k_c.reshape(batch, ctx_len, GQA_KV_WIDTH)
        v_c3 = v_c.reshape(batch, ctx_len, GQA_KV_WIDTH)
        k_all = jnp.concatenate([k_c3, k_l.reshape(batch, seq_len, GQA_KV_WIDTH)], axis=1)
        v_all = jnp.concatenate([v_c3, v_l.reshape(batch, seq_len, GQA_KV_WIDTH)], axis=1)
        y_gqa = _gqa(q_l, k_all, v_all, q_norm[l], k_norm[l], seq_len, tq, tk)
        y_na = _na(na_l, na_c, na_bias[l], batch, seq_len, ctx_len)
        x1 = _outproj(p_l, y_na, y_gqa, xl, mod, lat_row, w_out_p[l], pool_w_bd[l], ps, g1, b1, seq_len, tm)
        xl = _ffn(x1, mod, lat_row, w_up_b[l], conv_w[l], cb, w_down_b[l], g2, b2, seq_len, tm)

        if l < depth - 1:
            y_na_c = _ctx_mha(na_c, ctx_len)
            y_gqa_c = _gqa(q_c, k_c3, v_c3, q_norm[l], k_norm[l], ctx_len, ctx_len, ctx_len)
            xc1 = _outproj(p_c, y_na_c, y_gqa_c, xc, mod, ctx_row, w_out_p[l], pool_w_bd[l], ps, g1, b1,
                           ctx_len, tm_c)
            xc = _ffn(xc1, mod, ctx_row, w_up_b[l], conv_w[l], cb, w_down_b[l], g2, b2, ctx_len, tm_c)
    return xl.reshape(batch, seq_len, d)
```

```python
import functools

import jax
import jax.numpy as jnp
from jax import lax
from jax.experimental import pallas as pl
from jax.experimental.pallas import tpu as pltpu

F32 = jnp.float32
BF16 = jnp.bfloat16

D_MODEL = 1024
GRID_W = 64
HEAD_DIM = 64
LANES = 128
POOL_WIDTH = 256
POOL_WINDOWS = (2, 4, 8, 16)
POOL_HALO = 8
NA_WIDTH = 384
NA_HEADS = 6
NA_ROWS = 8
NA_COLS = 16
GQA_WIDTH = 384
GQA_KV_WIDTH = 128
D_FF = 2816
FF_CHUNK = 2816
ROPE_THETA = 10000.0
DEEPNORM_ALPHA = 8.0 ** 0.25
LN_EPS = 1e-6
MASK_VALUE = -1e30
LOG2_E = 1.4426950408889634
VMEM_LIMIT = 56 * 1024 * 1024

_C_POOL = 0
_C_NA = POOL_WIDTH
_C_GQ = _C_NA + 3 * NA_WIDTH
_C_GK = _C_GQ + GQA_WIDTH
_C_GV = _C_GK + GQA_KV_WIDTH
IN_WIDTH = _C_GV + GQA_KV_WIDTH


def _params(n_axes):
    return pltpu.CompilerParams(dimension_semantics=("parallel",) * n_axes,
                                vmem_limit_bytes=VMEM_LIMIT)


def _const_spec(shape):
    return pl.BlockSpec(shape, lambda *_: (0,) * len(shape), pipeline_mode=pl.Buffered(1))


def _dot(a, b):
    return jnp.dot(a, b, preferred_element_type=F32)


def _dot_nt(a, b):
    return lax.dot_general(a, b, (((1,), (1,)), ((), ())), preferred_element_type=F32)


def _layer_norm(y, g, b):
    mu = jnp.mean(y, axis=-1, keepdims=True)
    d = y - mu
    var = jnp.mean(d * d, axis=-1, keepdims=True)
    return d * lax.rsqrt(var + LN_EPS) * g + b


def _mod_kernel(c_ref, w_ref, b_ref, o_ref):
    c = c_ref[...]
    act = c * jax.nn.sigmoid(c)
    o_ref[0] = _dot(act, w_ref[0]) + b_ref[0]


def _modulation(c_rows, w_mod, b_mod):
    depth, d, n = w_mod.shape
    rows = c_rows.shape[0]
    bn = 1536
    return pl.pallas_call(
        _mod_kernel,
        out_shape=jax.ShapeDtypeStruct((depth, rows, n), F32),
        grid=(depth, n // bn),
        in_specs=[pl.BlockSpec((rows, d), lambda l, j: (0, 0)),
                  pl.BlockSpec((1, d, bn), lambda l, j: (l, 0, j)),
                  pl.BlockSpec((1, 1, bn), lambda l, j: (l, 0, j))],
        out_specs=pl.BlockSpec((1, rows, bn), lambda l, j: (l, 0, j)),
        compiler_params=_params(2),
        name="modulation",
    )(c_rows, w_mod, b_mod.reshape(depth, 1, n))


def _swap16(v):
    lane = lax.broadcasted_iota(jnp.int32, v.shape, 1)
    even = jnp.bitwise_and(lane, 16) == 0
    return jnp.where(even, pltpu.roll(v, LANES - 16, 1), pltpu.roll(v, 16, 1))


def _head_sumsq(z):
    width = z.shape[1]
    r = jnp.right_shift(lax.broadcasted_iota(jnp.int32, (width, width), 0), 6)
    c = jnp.right_shift(lax.broadcasted_iota(jnp.int32, (width, width), 1), 6)
    return _dot((z * z).astype(BF16), (r == c).astype(BF16))


def _norm_rope(z, ss, gain, cos, sin):
    zn = z * lax.rsqrt(ss * (1.0 / HEAD_DIM) + LN_EPS) * gain
    return zn * cos + _swap16(zn) * sin


def _inproj_kernel(x_ref, sh_ref, sc_ref, w_ref, cos_ref, sin_ref, qg_ref, kg_ref,
                   pool_ref, na_ref, q_ref, k_ref, v_ref):
    h = (x_ref[...] * (1.0 + sc_ref[0]) + sh_ref[0]).astype(BF16)
    pool_ref[...] = _dot(h, w_ref[:, _C_POOL:_C_NA])
    na_ref[:, 0:NA_WIDTH] = (_dot(h, w_ref[:, _C_NA:_C_NA + NA_WIDTH])
                             * (HEAD_DIM ** -0.5 * LOG2_E)).astype(BF16)
    na_ref[:, NA_WIDTH:3 * NA_WIDTH] = _dot(h, w_ref[:, _C_NA + NA_WIDTH:_C_GQ]).astype(BF16)
    cos = cos_ref[...]
    sin = sin_ref[...]
    zqk = _dot(h, w_ref[:, _C_GQ:_C_GV])
    ss = jnp.concatenate([_head_sumsq(zqk[:, 0:2 * LANES]), _head_sumsq(zqk[:, 2 * LANES:4 * LANES])], axis=1)
    for j in range(GQA_WIDTH // LANES):
        cols = slice(j * LANES, (j + 1) * LANES)
        blk = _norm_rope(zqk[:, cols], ss[:, cols], qg_ref[...], cos, sin)
        q_ref[:, cols] = (blk * (HEAD_DIM ** -0.5 * LOG2_E)).astype(BF16)
    cols = slice(GQA_WIDTH, GQA_WIDTH + GQA_KV_WIDTH)
    k_ref[...] = _norm_rope(zqk[:, cols], ss[:, cols], kg_ref[...], cos, sin).astype(BF16)
    v_ref[...] = _dot(h, w_ref[:, _C_GV:IN_WIDTH]).astype(BF16)


def _inproj(x2d, mod, mod_row, w_in, cos, sin, qg, kg, seq_len, tm):
    n = x2d.shape[0]
    tiles = seq_len // tm
    tok = lambda i: (i, 0)
    outs = (jax.ShapeDtypeStruct((n, POOL_WIDTH), F32),
            jax.ShapeDtypeStruct((n, 3 * NA_WIDTH), BF16),
            jax.ShapeDtypeStruct((n, GQA_WIDTH), BF16),
            jax.ShapeDtypeStruct((n, GQA_KV_WIDTH), BF16),
            jax.ShapeDtypeStruct((n, GQA_KV_WIDTH), BF16))
    return pl.pallas_call(
        _inproj_kernel,
        out_shape=outs,
        grid=(n // tm,),
        in_specs=[pl.BlockSpec((tm, D_MODEL), tok),
                  pl.BlockSpec((1, 1, D_MODEL), lambda i: (mod_row(i // tiles), 0, 0)),
                  pl.BlockSpec((1, 1, D_MODEL), lambda i: (mod_row(i // tiles), 0, 1)),
                  _const_spec((D_MODEL, IN_WIDTH)),
                  pl.BlockSpec((tm, LANES), lambda i: (i % tiles, 0)),
                  pl.BlockSpec((tm, LANES), lambda i: (i % tiles, 0)),
                  _const_spec((1, LANES)),
                  _const_spec((1, LANES))],
        out_specs=[pl.BlockSpec((tm, POOL_WIDTH), tok),
                   pl.BlockSpec((tm, 3 * NA_WIDTH), tok),
                   pl.BlockSpec((tm, GQA_WIDTH), tok),
                   pl.BlockSpec((tm, GQA_KV_WIDTH), tok),
                   pl.BlockSpec((tm, GQA_KV_WIDTH), tok)],
        compiler_params=_params(1),
        name="inproj",
    )(x2d, mod, mod, w_in, cos, sin, qg, kg)


def _gqa_kernel(q_ref, k_ref, vt_ref, o_ref, *, tq, bounded):
    n_k = k_ref.shape[1]
    row = lax.broadcasted_iota(jnp.int32, (LANES, tq), 0)
    low = row < HEAD_DIM
    blocks = [q_ref[:, j * LANES:(j + 1) * LANES].astype(F32).T for j in range(3)]
    qt = jnp.concatenate([jnp.where(low, b, 0.0) for b in blocks]
                         + [jnp.where(low, 0.0, b) for b in blocks], axis=1).astype(BF16)
    heads = 2 * 3

    def pv(i, pb):
        outs = []
        for h in range(heads):
            g = h // 3
            vt = vt_ref[0, i, g * HEAD_DIM:(g + 1) * HEAD_DIM, :]
            outs.append(_dot(vt, pb[:, h * tq:(h + 1) * tq]))
        return outs

    def bounded_body(i, carry):
        lsum, acc = carry
        p = jnp.exp2(_dot(k_ref[0, i], qt))
        lsum = lsum + jnp.sum(p, axis=0, keepdims=True)
        return lsum, acc + jnp.concatenate(pv(i, p.astype(BF16)), axis=0)

    def online_body(i, carry):
        m, l, acc = carry
        s = _dot(k_ref[0, i], qt)
        m_new = jnp.maximum(m, jnp.max(s, axis=0, keepdims=True))
        p = jnp.exp2(s - m_new)
        a = jnp.exp2(m - m_new)
        l = a * l + jnp.sum(p, axis=0, keepdims=True)
        outs = pv(i, p.astype(BF16))
        acc = jnp.concatenate([acc[h * HEAD_DIM:(h + 1) * HEAD_DIM] * a[:, h * tq:(h + 1) * tq] + outs[h]
                               for h in range(heads)], axis=0)
        return m_new, l, acc

    acc0 = jnp.zeros((heads * HEAD_DIM, tq), F32)
    l0 = jnp.zeros((1, heads * tq), F32)
    if bounded:
        l, acc = lax.fori_loop(0, n_k, bounded_body, (l0, acc0))
    else:
        m0 = jnp.full((1, heads * tq), MASK_VALUE, F32)
        _, l, acc = lax.fori_loop(0, n_k, online_body, (m0, l0, acc0))
    out = jnp.concatenate([acc[h * HEAD_DIM:(h + 1) * HEAD_DIM] / l[:, h * tq:(h + 1) * tq]
                           for h in range(heads)], axis=0)
    o_ref[...] = out.T.astype(BF16)


SCORE_BOUND = 60.0


def _gqa(q2d, k3d, v3d, gain_q, gain_k, seq_len, tq, tk):
    batch, lk, _ = k3d.shape
    k4d = k3d.reshape(batch, lk // tk, tk, GQA_KV_WIDTH)
    vt4d = v3d.reshape(batch, lk // tk, tk, GQA_KV_WIDTH).transpose(0, 1, 3, 2)
    bound = HEAD_DIM ** 0.5 * jnp.max(jnp.abs(gain_q)) * jnp.max(jnp.abs(gain_k))
    return lax.cond(bound <= SCORE_BOUND,
                    lambda: _gqa_call(q2d, k4d, vt4d, seq_len, tq, True),
                    lambda: _gqa_call(q2d, k4d, vt4d, seq_len, tq, False))


def _gqa_call(q2d, k4d, vt4d, seq_len, tq, bounded):
    n = q2d.shape[0]
    _, n_k, tk, _ = k4d.shape
    tiles = seq_len // tq
    return pl.pallas_call(
        functools.partial(_gqa_kernel, tq=tq, bounded=bounded),
        out_shape=jax.ShapeDtypeStruct((n, GQA_WIDTH), BF16),
        grid=(n // tq,),
        in_specs=[pl.BlockSpec((tq, GQA_WIDTH), lambda i: (i, 0)),
                  pl.BlockSpec((1, n_k, tk, GQA_KV_WIDTH), lambda i: (i // tiles, 0, 0, 0)),
                  pl.BlockSpec((1, n_k, GQA_KV_WIDTH, tk), lambda i: (i // tiles, 0, 0, 0))],
        out_specs=pl.BlockSpec((tq, GQA_WIDTH), lambda i: (i, 0)),
        compiler_params=_params(1),
        name="gqa",
    )(q2d, k4d, vt4d)


def _pair_queries(qb, low):
    zero = jnp.zeros_like(qb)
    return jnp.concatenate([jnp.where(low, qb, zero), jnp.where(low, zero, qb)], axis=0)


def _ctx_mha_kernel(q_ref, k_ref, v_ref, o_ref):
    rows = q_ref.shape[0]
    lane = lax.broadcasted_iota(jnp.int32, (rows, LANES), 1)
    low = lane < HEAD_DIM
    for j in range(NA_WIDTH // LANES):
        cols = slice(j * LANES, (j + 1) * LANES)
        qx = _pair_queries(q_ref[:, cols], low)
        s = _dot_nt(qx, k_ref[:, cols])
        p = jnp.exp2(s - jnp.max(s, axis=1, keepdims=True))
        l = jnp.sum(p, axis=1, keepdims=True)
        o = _dot(p.astype(BF16), v_ref[:, cols]) / l
        o_ref[:, cols] = jnp.where(low, o[:rows], o[rows:]).astype(BF16)


def _ctx_mha(na_c, ctx_len):
    n = na_c.shape[0]
    return pl.pallas_call(
        _ctx_mha_kernel,
        out_shape=jax.ShapeDtypeStruct((n, NA_WIDTH), BF16),
        grid=(n // ctx_len,),
        in_specs=[pl.BlockSpec((ctx_len, NA_WIDTH), lambda b: (b, 0)),
                  pl.BlockSpec((ctx_len, NA_WIDTH), lambda b: (b, 1)),
                  pl.BlockSpec((ctx_len, NA_WIDTH), lambda b: (b, 2))],
        out_specs=pl.BlockSpec((ctx_len, NA_WIDTH), lambda b: (b, 0)),
        compiler_params=_params(1),
        name="ctx_mha",
    )(na_c, na_c, na_c)


NA_BLOCK_ROWS = 8
NA_BLOCK = NA_BLOCK_ROWS * GRID_W
NA_BAND = NA_ROWS * GRID_W


def _na_kernel(q_ref, kp_ref, kc_ref, kn_ref, vp_ref, vc_ref, vn_ref, kx_ref, vx_ref, bias_ref,
               o_ref, kband, vband, *, n_rows):
    i = pl.program_id(1)
    kband[0:NA_BLOCK] = kp_ref[...]
    kband[NA_BLOCK:2 * NA_BLOCK] = kc_ref[...]
    kband[2 * NA_BLOCK:3 * NA_BLOCK] = kn_ref[...]
    vband[0:NA_BLOCK] = vp_ref[...]
    vband[NA_BLOCK:2 * NA_BLOCK] = vc_ref[...]
    vband[2 * NA_BLOCK:3 * NA_BLOCK] = vn_ref[...]
    lane = lax.broadcasted_iota(jnp.int32, (GRID_W, LANES), 1)
    low = lane < HEAD_DIM

    for j in range(NA_BLOCK_ROWS):
        r = i * NA_BLOCK_ROWS + j
        r_start = jnp.clip(r - NA_ROWS // 2, 0, n_rows - NA_ROWS)
        shift = r - r_start
        off = pl.multiple_of((r_start - (i - 1) * NA_BLOCK_ROWS) * GRID_W, GRID_W)
        qrows = slice(j * GRID_W, (j + 1) * GRID_W)
        for p in range(NA_WIDTH // LANES):
            cols = slice(p * LANES, (p + 1) * LANES)
            qx = _pair_queries(q_ref[qrows, cols], low)
            s_nb = _dot_nt(qx, kband[pl.ds(off, NA_BAND), cols]) + bias_ref[p, shift]
            s_cx = _dot_nt(qx, kx_ref[:, cols])
            m = jnp.maximum(jnp.max(s_nb, axis=1, keepdims=True), jnp.max(s_cx, axis=1, keepdims=True))
            p_nb = jnp.exp2(s_nb - m)
            p_cx = jnp.exp2(s_cx - m)
            l = jnp.sum(p_nb, axis=1, keepdims=True) + jnp.sum(p_cx, axis=1, keepdims=True)
            o = (_dot(p_nb.astype(BF16), vband[pl.ds(off, NA_BAND), cols])
                 + _dot(p_cx.astype(BF16), vx_ref[:, cols])) / l
            o_ref[qrows, cols] = jnp.where(low, o[:GRID_W], o[GRID_W:]).astype(BF16)


def _na_bias_table(rpb):
    depth = rpb.shape[0]
    col = jnp.arange(GRID_W)
    c_start = jnp.clip(col - NA_COLS // 2, 0, GRID_W - NA_COLS)
    in_win = (col[None, :] >= c_start[:, None]) & (col[None, :] < c_start[:, None] + NA_COLS)
    dc = col[None, :] - col[:, None] + NA_COLS - 1
    sel_c = ((dc[:, :, None] == jnp.arange(2 * NA_COLS - 1)) & in_win[:, :, None]).astype(F32)
    dr = jnp.arange(NA_ROWS)[None, :] - jnp.arange(NA_ROWS)[:, None] + NA_ROWS - 1
    sel_r = (dr[:, :, None] == jnp.arange(2 * NA_ROWS - 1)).astype(F32)
    pairs = rpb.astype(F32).reshape(depth, NA_HEADS // 2, 2, 2 * NA_ROWS - 1, 2 * NA_COLS - 1) * LOG2_E
    t = jnp.einsum("sir,lpurd,ckd->lpsucik", sel_r, pairs, sel_c, precision=lax.Precision.HIGHEST)
    t = jnp.where(in_win[:, None, :], t, MASK_VALUE)
    return t.reshape(depth, NA_HEADS // 2, NA_ROWS, 2 * GRID_W, NA_BAND)


def _na(na, na_c, bias, batch, seq_len, ctx_len):
    n = na.shape[0]
    n_rows = seq_len // GRID_W
    blocks = seq_len // NA_BLOCK
    tok = lambda col: (lambda b, i: (b * blocks + i, col))
    prev = lambda col: (lambda b, i: (b * blocks + jnp.maximum(i - 1, 0), col))
    nxt = lambda col: (lambda b, i: (b * blocks + jnp.minimum(i + 1, blocks - 1), col))
    blk = lambda f: pl.BlockSpec((NA_BLOCK, NA_WIDTH), f)
    return pl.pallas_call(
        functools.partial(_na_kernel, n_rows=n_rows),
        out_shape=jax.ShapeDtypeStruct((n, NA_WIDTH), BF16),
        grid=(batch, blocks),
        in_specs=[blk(tok(0)),
                  blk(prev(1)), blk(tok(1)), blk(nxt(1)),
                  blk(prev(2)), blk(tok(2)), blk(nxt(2)),
                  pl.BlockSpec((ctx_len, NA_WIDTH), lambda b, i: (b, 1)),
                  pl.BlockSpec((ctx_len, NA_WIDTH), lambda b, i: (b, 2)),
                  _const_spec(bias.shape)],
        out_specs=blk(tok(0)),
        scratch_shapes=[pltpu.VMEM((3 * NA_BLOCK, NA_WIDTH), BF16),
                        pltpu.VMEM((3 * NA_BLOCK, NA_WIDTH), BF16)],
        compiler_params=_params(2),
        name="natten",
    )(na, na, na, na, na, na, na, na_c, na_c, bias)


def _outproj_kernel(pp_ref, pm_ref, pn_ref, yna_ref, ygq_ref, x_ref, g1_ref, wout_ref, pw_ref, ps_ref,
                    lng_ref, lnb_ref, o_ref, *, seq_len, tm):
    tiles = seq_len // tm
    ti = pl.program_id(0) % tiles
    prev = jnp.where(ti == 0, 0.0, pp_ref[...])
    nxt = jnp.where(ti == tiles - 1, 0.0, pn_ref[...])
    u = jnp.concatenate([prev, pm_ref[...], nxt], axis=0)
    n = tm + 2 * POOL_HALO
    s2 = u + pltpu.roll(u, 1, 0)
    s4 = pltpu.roll(s2, 1, 0) + pltpu.roll(s2, n - 1, 0)
    s8 = pltpu.roll(s4, 2, 0) + pltpu.roll(s4, n - 2, 0)
    s16 = pltpu.roll(s8, 4, 0) + pltpu.roll(s8, n - 4, 0)
    mid = slice(POOL_HALO, POOL_HALO + tm)
    pos = ti * tm + lax.broadcasted_iota(jnp.int32, (tm, POOL_WIDTH), 0)
    grp = jnp.right_shift(lax.broadcasted_iota(jnp.int32, (tm, POOL_WIDTH), 1), 6)
    half = jnp.left_shift(1, grp)
    cnt = jnp.minimum(pos + half - 1, seq_len - 1) - jnp.maximum(pos - half, 0) + 1
    wsum = jnp.where(grp == 0, s2[mid], jnp.where(grp == 1, s4[mid], jnp.where(grp == 2, s8[mid], s16[mid])))
    pooled = wsum / cnt.astype(F32) - u[mid]
    y_pool = _dot(pooled.astype(BF16), pw_ref[...]) * ps_ref[...]
    mix = (_dot(y_pool.astype(BF16), wout_ref[0:POOL_WIDTH])
           + _dot(yna_ref[...], wout_ref[POOL_WIDTH:POOL_WIDTH + NA_WIDTH])
           + _dot(ygq_ref[...], wout_ref[POOL_WIDTH + NA_WIDTH:D_MODEL]))
    y = DEEPNORM_ALPHA * x_ref[...] + g1_ref[0] * mix
    o_ref[...] = _layer_norm(y, lng_ref[...], lnb_ref[...])


def _outproj(pool_in, y_na, y_gqa, x2d, mod, mod_row, w_out, pool_w_bd, pool_scale, ln_g, ln_b, seq_len, tm):
    n = x2d.shape[0]
    tiles = seq_len // tm
    hb = tm // POOL_HALO
    last_hb = n // POOL_HALO - 1
    tok = lambda i: (i, 0)
    return pl.pallas_call(
        functools.partial(_outproj_kernel, seq_len=seq_len, tm=tm),
        out_shape=jax.ShapeDtypeStruct((n, D_MODEL), F32),
        grid=(n // tm,),
        in_specs=[pl.BlockSpec((POOL_HALO, POOL_WIDTH), lambda i: (jnp.maximum(i * hb - 1, 0), 0)),
                  pl.BlockSpec((tm, POOL_WIDTH), tok),
                  pl.BlockSpec((POOL_HALO, POOL_WIDTH), lambda i: (jnp.minimum((i + 1) * hb, last_hb), 0)),
                  pl.BlockSpec((tm, NA_WIDTH), tok),
                  pl.BlockSpec((tm, GQA_WIDTH), tok),
                  pl.BlockSpec((tm, D_MODEL), tok),
                  pl.BlockSpec((1, 1, D_MODEL), lambda i: (mod_row(i // tiles), 0, 2)),
                  _const_spec((D_MODEL, D_MODEL)),
                  _const_spec((POOL_WIDTH, POOL_WIDTH)),
                  _const_spec((1, POOL_WIDTH)),
                  _const_spec((1, D_MODEL)),
                  _const_spec((1, D_MODEL))],
        out_specs=pl.BlockSpec((tm, D_MODEL), tok),
        compiler_params=_params(1),
        name="outproj",
    )(pool_in, pool_in, pool_in, y_na, y_gqa, x2d, mod, w_out, pool_w_bd, pool_scale, ln_g, ln_b)


FFN_HALO = 8


def _ffn_kernel(xp_ref, xm_ref, xn_ref, sh_ref, sc_ref, g2_ref, wup_ref, cw_ref, cb_ref, wdn_ref,
                lng_ref, lnb_ref, o_ref, *, seq_len, tm):
    tiles = seq_len // tm
    ti = pl.program_id(0) % tiles
    scale = 1.0 + sc_ref[0]
    shift = sh_ref[0]
    xm = xm_ref[...]
    hp = jnp.where(ti == 0, 0.0, xp_ref[...] * scale + shift)
    hn = jnp.where(ti == tiles - 1, 0.0, xn_ref[...] * scale + shift)
    h = jnp.concatenate([hp, xm * scale + shift, hn], axis=0).astype(BF16)
    n = tm + 2 * FFN_HALO
    mid = slice(FFN_HALO, FFN_HALO + tm)

    def conv(u, c0):
        cw = cw_ref[:, c0:c0 + FF_CHUNK]
        y = (pltpu.roll(u, 1, 0) * cw[0:1] + cb_ref[:, c0:c0 + FF_CHUNK]
             + u * cw[1:2] + pltpu.roll(u, n - 1, 0) * cw[2:3])
        return y[mid]

    acc = jnp.zeros((tm, D_MODEL), F32)
    for ch in range(D_FF // FF_CHUNK):
        a0 = ch * FF_CHUNK
        g0 = D_FF + a0
        a = conv(_dot(h, wup_ref[:, a0:a0 + FF_CHUNK]), a0)
        g = conv(_dot(h, wup_ref[:, g0:g0 + FF_CHUNK]), g0)
        act = (a * (g * jax.nn.sigmoid(g))).astype(BF16)
        acc = acc + _dot(act, wdn_ref[a0:a0 + FF_CHUNK])
    y = DEEPNORM_ALPHA * xm + g2_ref[0] * acc
    o_ref[...] = _layer_norm(y, lng_ref[...], lnb_ref[...])


def _ffn(x2d, mod, mod_row, w_up, conv_w, conv_b, w_down, ln_g, ln_b, seq_len, tm):
    n = x2d.shape[0]
    tiles = seq_len // tm
    hb = tm // FFN_HALO
    last_hb = n // FFN_HALO - 1
    tok = lambda i: (i, 0)
    mod_spec = lambda k: pl.BlockSpec((1, 1, D_MODEL), lambda i: (mod_row(i // tiles), 0, k))
    return pl.pallas_call(
        functools.partial(_ffn_kernel, seq_len=seq_len, tm=tm),
        out_shape=jax.ShapeDtypeStruct((n, D_MODEL), F32),
        grid=(n // tm,),
        in_specs=[pl.BlockSpec((FFN_HALO, D_MODEL), lambda i: (jnp.maximum(i * hb - 1, 0), 0)),
                  pl.BlockSpec((tm, D_MODEL), tok),
                  pl.BlockSpec((FFN_HALO, D_MODEL), lambda i: (jnp.minimum((i + 1) * hb, last_hb), 0)),
                  mod_spec(3), mod_spec(4), mod_spec(5),
                  _const_spec((D_MODEL, 2 * D_FF)),
                  _const_spec((3, 2 * D_FF)),
                  _const_spec((1, 2 * D_FF)),
                  _const_spec((D_FF, D_MODEL)),
                  _const_spec((1, D_MODEL)),
                  _const_spec((1, D_MODEL))],
        out_specs=pl.BlockSpec((tm, D_MODEL), tok),
        compiler_params=_params(1),
        name="convffn",
    )(x2d, x2d, x2d, mod, mod, mod, w_up, conv_w, conv_b, w_down, ln_g, ln_b)


def _rope_tables(seq_len):
    t = jnp.arange(seq_len, dtype=jnp.int32)
    half = HEAD_DIM // 2
    inv = ROPE_THETA ** (-jnp.arange(0, half, 2, dtype=F32) / half)
    ang_r = (t // GRID_W).astype(F32)[:, None] * inv
    ang_c = (t % GRID_W).astype(F32)[:, None] * inv
    cos = jnp.concatenate([jnp.cos(ang_r)] * 2 + [jnp.cos(ang_c)] * 2, axis=-1)
    sin = jnp.concatenate([-jnp.sin(ang_r), jnp.sin(ang_r), -jnp.sin(ang_c), jnp.sin(ang_c)], axis=-1)
    return jnp.tile(cos, (1, 2)), jnp.tile(sin, (1, 2))


def _pair_head_order():
    heads = jnp.array([0, 3, 1, 4, 2, 5])
    return (heads[:, None] * HEAD_DIM + jnp.arange(HEAD_DIM)[None, :]).reshape(-1)


def kernel(x, c, ctx, c_ctx, w_mod, b_mod, w_in, pool_w, pool_scale, na_rpb, q_norm, k_norm, w_out,
           ln1_g, ln1_b, w_up, conv_w, conv_b, w_down, ln2_g, ln2_b):
    batch, seq_len, d = x.shape
    ctx_len = ctx.shape[1]
    depth = w_mod.shape[0]
    assert d == D_MODEL and seq_len % NA_BLOCK == 0 and seq_len // GRID_W >= NA_ROWS
    tm = 512
    tm_c = ctx_len
    tq = 256
    tk = next(t for t in (2816, 768, 256) if (seq_len + ctx_len) % t == 0)
    assert (seq_len + ctx_len) % tk == 0 and seq_len % tq == 0 and ctx_len % 256 == 0

    mod_rows = -(-(batch + 1) // 8) * 8
    c_rows = jnp.zeros((mod_rows, d), F32).at[:batch].set(c).at[batch].set(c_ctx)
    mods = _modulation(c_rows, w_mod, b_mod)
    lat_row = lambda b: b
    ctx_row = lambda b: batch

    order = _pair_head_order()
    q_cols = _C_GQ + order
    w_in_p = jnp.concatenate([w_in[:, :, :_C_GQ], w_in[:, :, q_cols], w_in[:, :, _C_GK:]], axis=-1).astype(BF16)
    w_out_p = w_out.astype(BF16)
    w_up_b = w_up.astype(BF16)
    w_down_b = w_down.astype(BF16)
    eye = jnp.eye(pool_w.shape[1], dtype=F32)
    pool_w_bd = jnp.einsum("gh,lgcd->lgchd", eye, pool_w).reshape(depth, POOL_WIDTH, POOL_WIDTH).astype(BF16)

    na_bias = _na_bias_table(na_rpb)
    cos, sin = _rope_tables(seq_len)
    cos_c = jnp.ones((ctx_len, LANES), F32)
    sin_c = jnp.zeros((ctx_len, LANES), F32)

    xl = x.reshape(batch * seq_len, d)
    xc = ctx.reshape(batch * ctx_len, d)
    for l in range(depth):
        mod = mods[l].reshape(mod_rows, 1, 6 * d)
        qg = jnp.tile(q_norm[l], 2).reshape(1, LANES)
        kg = jnp.tile(k_norm[l], 2).reshape(1, LANES)
        ps = pool_scale[l].reshape(1, POOL_WIDTH)
        g1, b1 = ln1_g[l].reshape(1, d), ln1_b[l].reshape(1, d)
        g2, b2 = ln2_g[l].reshape(1, d), ln2_b[l].reshape(1, d)
        cb = conv_b[l].reshape(1, 2 * D_FF)

        p_c, na_c, q_c, k_c, v_c = _inproj(xc, mod, ctx_row, w_in_p[l], cos_c, sin_c, qg, kg, ctx_len, tm_c)
        p_l, na_l, q_l, k_l, v_l = _inproj(xl, mod, lat_row, w_in_p[l], cos, sin, qg, kg, seq_len, tm)

        k_c3 = k_c.reshape(batch, ctx_len, GQA_KV_WIDTH)
        v_c3 = v_c.reshape(batch, ctx_len, GQA_KV_WIDTH)
        k_all = jnp.concatenate([k_c3, k_l.reshape(batch, seq_len, GQA_KV_WIDTH)], axis=1)
        v_all = jnp.concatenate([v_c3, v_l.reshape(batch, seq_len, GQA_KV_WIDTH)], axis=1)
        y_gqa = _gqa(q_l, k_all, v_all, q_norm[l], k_norm[l], seq_len, tq, tk)
        y_na = _na(na_l, na_c, na_bias[l], batch, seq_len, ctx_len)
        x1 = _outproj(p_l, y_na, y_gqa, xl, mod, lat_row, w_out_p[l], pool_w_bd[l], ps, g1, b1, seq_len, tm)
        xl = _ffn(x1, mod, lat_row, w_up_b[l], conv_w[l], cb, w_down_b[l], g2, b2, seq_len, tm)

        if l < depth - 1:
            y_na_c = _ctx_mha(na_c, ctx_len)
            y_gqa_c = _gqa(q_c, k_c3, v_c3, q_norm[l], k_norm[l], ctx_len, ctx_len, ctx_len)
            xc1 = _outproj(p_c, y_na_c, y_gqa_c, xc, mod, ctx_row, w_out_p[l], pool_w_bd[l], ps, g1, b1,
                           ctx_len, tm_c)
            xc = _ffn(xc1, mod, ctx_row, w_up_b[l], conv_w[l], cb, w_down_b[l], g2, b2, ctx_len, tm_c)
    return xl.reshape(batch, seq_len, d)
```

```python
import functools

import jax
import jax.numpy as jnp
from jax import lax
from jax.experimental import pallas as pl
from jax.experimental.pallas import tpu as pltpu

F32 = jnp.float32
BF16 = jnp.bfloat16

D_MODEL = 1024
GRID_W = 64
HEAD_DIM = 64
LANES = 128
POOL_WIDTH = 256
POOL_WINDOWS = (2, 4, 8, 16)
POOL_HALO = 8
NA_WIDTH = 384
NA_HEADS = 6
NA_ROWS = 8
NA_COLS = 16
GQA_WIDTH = 384
GQA_KV_WIDTH = 128
D_FF = 2816
FF_CHUNK = 2816
ROPE_THETA = 10000.0
DEEPNORM_ALPHA = 8.0 ** 0.25
LN_EPS = 1e-6
MASK_VALUE = -1e30
LOG2_E = 1.4426950408889634
VMEM_LIMIT = 56 * 1024 * 1024

_C_POOL = 0
_C_NA = POOL_WIDTH
_C_GQ = _C_NA + 3 * NA_WIDTH
_C_GK = _C_GQ + GQA_WIDTH
_C_GV = _C_GK + GQA_KV_WIDTH
IN_WIDTH = _C_GV + GQA_KV_WIDTH


def _params(n_axes):
    return pltpu.CompilerParams(dimension_semantics=("parallel",) * n_axes,
                                vmem_limit_bytes=VMEM_LIMIT)


def _const_spec(shape):
    return pl.BlockSpec(shape, lambda *_: (0,) * len(shape), pipeline_mode=pl.Buffered(1))


def _dot(a, b):
    return jnp.dot(a, b, preferred_element_type=F32)


def _dot_nt(a, b):
    return lax.dot_general(a, b, (((1,), (1,)), ((), ())), preferred_element_type=F32)


def _layer_norm(y, g, b):
    mu = jnp.mean(y, axis=-1, keepdims=True)
    d = y - mu
    var = jnp.mean(d * d, axis=-1, keepdims=True)
    return d * lax.rsqrt(var + LN_EPS) * g + b


def _mod_kernel(c_ref, w_ref, b_ref, o_ref):
    c = c_ref[...]
    act = c * jax.nn.sigmoid(c)
    o_ref[0] = _dot(act, w_ref[0]) + b_ref[0]


def _modulation(c_rows, w_mod, b_mod):
    depth, d, n = w_mod.shape
    rows = c_rows.shape[0]
    bn = 1536
    return pl.pallas_call(
        _mod_kernel,
        out_shape=jax.ShapeDtypeStruct((depth, rows, n), F32),
        grid=(depth, n // bn),
        in_specs=[pl.BlockSpec((rows, d), lambda l, j: (0, 0)),
                  pl.BlockSpec((1, d, bn), lambda l, j: (l, 0, j)),
                  pl.BlockSpec((1, 1, bn), lambda l, j: (l, 0, j))],
        out_specs=pl.BlockSpec((1, rows, bn), lambda l, j: (l, 0, j)),
        compiler_params=_params(2),
        name="modulation",
    )(c_rows, w_mod, b_mod.reshape(depth, 1, n))


def _swap16(v):
    lane = lax.broadcasted_iota(jnp.int32, v.shape, 1)
    even = jnp.bitwise_and(lane, 16) == 0
    return jnp.where(even, pltpu.roll(v, LANES - 16, 1), pltpu.roll(v, 16, 1))


def _head_sumsq(z):
    width = z.shape[1]
    r = jnp.right_shift(lax.broadcasted_iota(jnp.int32, (width, width), 0), 6)
    c = jnp.right_shift(lax.broadcasted_iota(jnp.int32, (width, width), 1), 6)
    return _dot((z * z).astype(BF16), (r == c).astype(BF16))


def _norm_rope(z, ss, gain, cos, sin):
    zn = z * lax.rsqrt(ss * (1.0 / HEAD_DIM) + LN_EPS) * gain
    return zn * cos + _swap16(zn) * sin


def _inproj_kernel(x_ref, sh_ref, sc_ref, w_ref, cos_ref, sin_ref, qg_ref, kg_ref,
                   pool_ref, na_ref, q_ref, k_ref, v_ref):
    h = (x_ref[...] * (1.0 + sc_ref[0]) + sh_ref[0]).astype(BF16)
    pool_ref[...] = _dot(h, w_ref[:, _C_POOL:_C_NA])
    na_ref[:, 0:NA_WIDTH] = (_dot(h, w_ref[:, _C_NA:_C_NA + NA_WIDTH])
                             * (HEAD_DIM ** -0.5 * LOG2_E)).astype(BF16)
    na_ref[:, NA_WIDTH:3 * NA_WIDTH] = _dot(h, w_ref[:, _C_NA + NA_WIDTH:_C_GQ]).astype(BF16)
    cos = cos_ref[...]
    sin = sin_ref[...]
    zqk = _dot(h, w_ref[:, _C_GQ:_C_GV])
    ss = jnp.concatenate([_head_sumsq(zqk[:, 0:2 * LANES]), _head_sumsq(zqk[:, 2 * LANES:4 * LANES])], axis=1)
    for j in range(GQA_WIDTH // LANES):
        cols = slice(j * LANES, (j + 1) * LANES)
        blk = _norm_rope(zqk[:, cols], ss[:, cols], qg_ref[...], cos, sin)
        q_ref[:, cols] = (blk * (HEAD_DIM ** -0.5 * LOG2_E)).astype(BF16)
    cols = slice(GQA_WIDTH, GQA_WIDTH + GQA_KV_WIDTH)
    k_ref[...] = _norm_rope(zqk[:, cols], ss[:, cols], kg_ref[...], cos, sin).astype(BF16)
    v_ref[...] = _dot(h, w_ref[:, _C_GV:IN_WIDTH]).astype(BF16)


def _inproj(x2d, mod, mod_row, w_in, cos, sin, qg, kg, seq_len, tm):
    n = x2d.shape[0]
    tiles = seq_len // tm
    tok = lambda i: (i, 0)
    outs = (jax.ShapeDtypeStruct((n, POOL_WIDTH), F32),
            jax.ShapeDtypeStruct((n, 3 * NA_WIDTH), BF16),
            jax.ShapeDtypeStruct((n, GQA_WIDTH), BF16),
            jax.ShapeDtypeStruct((n, GQA_KV_WIDTH), BF16),
            jax.ShapeDtypeStruct((n, GQA_KV_WIDTH), BF16))
    return pl.pallas_call(
        _inproj_kernel,
        out_shape=outs,
        grid=(n // tm,),
        in_specs=[pl.BlockSpec((tm, D_MODEL), tok),
                  pl.BlockSpec((1, 1, D_MODEL), lambda i: (mod_row(i // tiles), 0, 0)),
                  pl.BlockSpec((1, 1, D_MODEL), lambda i: (mod_row(i // tiles), 0, 1)),
                  _const_spec((D_MODEL, IN_WIDTH)),
                  pl.BlockSpec((tm, LANES), lambda i: (i % tiles, 0)),
                  pl.BlockSpec((tm, LANES), lambda i: (i % tiles, 0)),
                  _const_spec((1, LANES)),
                  _const_spec((1, LANES))],
        out_specs=[pl.BlockSpec((tm, POOL_WIDTH), tok),
                   pl.BlockSpec((tm, 3 * NA_WIDTH), tok),
                   pl.BlockSpec((tm, GQA_WIDTH), tok),
                   pl.BlockSpec((tm, GQA_KV_WIDTH), tok),
                   pl.BlockSpec((tm, GQA_KV_WIDTH), tok)],
        compiler_params=_params(1),
        name="inproj",
    )(x2d, mod, mod, w_in, cos, sin, qg, kg)


def _gqa_kernel(q_ref, k_ref, vt_ref, o_ref, *, tq, bounded):
    n_k = k_ref.shape[1]
    row = lax.broadcasted_iota(jnp.int32, (LANES, tq), 0)
    low = row < HEAD_DIM
    blocks = [q_ref[:, j * LANES:(j + 1) * LANES].astype(F32).T for j in range(3)]
    qt = jnp.concatenate([jnp.where(low, b, 0.0) for b in blocks]
                         + [jnp.where(low, 0.0, b) for b in blocks], axis=1).astype(BF16)
    heads = 2 * 3

    def pv(i, pb):
        outs = []
        for h in range(heads):
            g = h // 3
            vt = vt_ref[0, i, g * HEAD_DIM:(g + 1) * HEAD_DIM, :]
            outs.append(_dot(vt, pb[:, h * tq:(h + 1) * tq]))
        return outs

    def bounded_body(i, carry):
        lsum, acc = carry
        p = jnp.exp2(_dot(k_ref[0, i], qt))
        lsum = lsum + jnp.sum(p, axis=0, keepdims=True)
        return lsum, acc + jnp.concatenate(pv(i, p.astype(BF16)), axis=0)

    def online_body(i, carry):
        m, l, acc = carry
        s = _dot(k_ref[0, i], qt)
        m_new = jnp.maximum(m, jnp.max(s, axis=0, keepdims=True))
        p = jnp.exp2(s - m_new)
        a = jnp.exp2(m - m_new)
        l = a * l + jnp.sum(p, axis=0, keepdims=True)
        outs = pv(i, p.astype(BF16))
        acc = jnp.concatenate([acc[h * HEAD_DIM:(h + 1) * HEAD_DIM] * a[:, h * tq:(h + 1) * tq] + outs[h]
                               for h in range(heads)], axis=0)
        return m_new, l, acc

    acc0 = jnp.zeros((heads * HEAD_DIM, tq), F32)
    l0 = jnp.zeros((1, heads * tq), F32)
    if bounded:
        l, acc = l0, acc0
        for i in range(n_k):
            l, acc = bounded_body(i, (l, acc))
    else:
        m0 = jnp.full((1, heads * tq), MASK_VALUE, F32)
        _, l, acc = lax.fori_loop(0, n_k, online_body, (m0, l0, acc0))
    out = jnp.concatenate([acc[h * HEAD_DIM:(h + 1) * HEAD_DIM] / l[:, h * tq:(h + 1) * tq]
                           for h in range(heads)], axis=0)
    o_ref[...] = out.T.astype(BF16)


SCORE_BOUND = 60.0


def _gqa(q2d, k3d, v3d, gain_q, gain_k, seq_len, tq, tk):
    batch, lk, _ = k3d.shape
    k4d = k3d.reshape(batch, lk // tk, tk, GQA_KV_WIDTH)
    vt4d = v3d.reshape(batch, lk // tk, tk, GQA_KV_WIDTH).transpose(0, 1, 3, 2)
    bound = HEAD_DIM ** 0.5 * jnp.max(jnp.abs(gain_q)) * jnp.max(jnp.abs(gain_k))
    return lax.cond(bound <= SCORE_BOUND,
                    lambda: _gqa_call(q2d, k4d, vt4d, seq_len, tq, True),
                    lambda: _gqa_call(q2d, k4d, vt4d, seq_len, tq, False))


def _gqa_call(q2d, k4d, vt4d, seq_len, tq, bounded):
    n = q2d.shape[0]
    _, n_k, tk, _ = k4d.shape
    tiles = seq_len // tq
    return pl.pallas_call(
        functools.partial(_gqa_kernel, tq=tq, bounded=bounded),
        out_shape=jax.ShapeDtypeStruct((n, GQA_WIDTH), BF16),
        grid=(n // tq,),
        in_specs=[pl.BlockSpec((tq, GQA_WIDTH), lambda i: (i, 0)),
                  pl.BlockSpec((1, n_k, tk, GQA_KV_WIDTH), lambda i: (i // tiles, 0, 0, 0)),
                  pl.BlockSpec((1, n_k, GQA_KV_WIDTH, tk), lambda i: (i // tiles, 0, 0, 0))],
        out_specs=pl.BlockSpec((tq, GQA_WIDTH), lambda i: (i, 0)),
        compiler_params=_params(1),
        name="gqa",
    )(q2d, k4d, vt4d)


def _pair_queries(qb, low):
    zero = jnp.zeros_like(qb)
    return jnp.concatenate([jnp.where(low, qb, zero), jnp.where(low, zero, qb)], axis=0)


def _ctx_mha_kernel(q_ref, k_ref, v_ref, o_ref):
    rows = q_ref.shape[0]
    lane = lax.broadcasted_iota(jnp.int32, (rows, LANES), 1)
    low = lane < HEAD_DIM
    for j in range(NA_WIDTH // LANES):
        cols = slice(j * LANES, (j + 1) * LANES)
        qx = _pair_queries(q_ref[:, cols], low)
        s = _dot_nt(qx, k_ref[:, cols])
        p = jnp.exp2(s - jnp.max(s, axis=1, keepdims=True))
        l = jnp.sum(p, axis=1, keepdims=True)
        o = _dot(p.astype(BF16), v_ref[:, cols]) / l
        o_ref[:, cols] = jnp.where(low, o[:rows], o[rows:]).astype(BF16)


def _ctx_mha(na_c, ctx_len):
    n = na_c.shape[0]
    return pl.pallas_call(
        _ctx_mha_kernel,
        out_shape=jax.ShapeDtypeStruct((n, NA_WIDTH), BF16),
        grid=(n // ctx_len,),
        in_specs=[pl.BlockSpec((ctx_len, NA_WIDTH), lambda b: (b, 0)),
                  pl.BlockSpec((ctx_len, NA_WIDTH), lambda b: (b, 1)),
                  pl.BlockSpec((ctx_len, NA_WIDTH), lambda b: (b, 2))],
        out_specs=pl.BlockSpec((ctx_len, NA_WIDTH), lambda b: (b, 0)),
        compiler_params=_params(1),
        name="ctx_mha",
    )(na_c, na_c, na_c)


NA_BLOCK_ROWS = 8
NA_BLOCK = NA_BLOCK_ROWS * GRID_W
NA_BAND = NA_ROWS * GRID_W
NA_GROUP = 2
NA_GBAND_ROWS = 10
NA_GBAND = NA_GBAND_ROWS * GRID_W
NA_PAD = NA_GBAND - NA_BAND


def _transpose_bf16(a):
    return a.astype(F32).T.astype(BF16)


def _na_kernel(q_ref, kp_ref, kc_ref, kn_ref, vp_ref, vc_ref, vn_ref, kx_ref, vx_ref, bias_ref,
               o_ref, kband, vband, *, n_rows):
    i = pl.program_id(1)
    kband[0:NA_BLOCK] = kp_ref[...]
    kband[NA_BLOCK:2 * NA_BLOCK] = kc_ref[...]
    kband[2 * NA_BLOCK:3 * NA_BLOCK] = kn_ref[...]
    vband[0:NA_BLOCK] = vp_ref[...]
    vband[NA_BLOCK:2 * NA_BLOCK] = vc_ref[...]
    vband[2 * NA_BLOCK:3 * NA_BLOCK] = vn_ref[...]
    lane = lax.broadcasted_iota(jnp.int32, (GRID_W, LANES), 1)
    low = lane < HEAD_DIM

    for p in range(NA_WIDTH // LANES):
        cols = slice(p * LANES, (p + 1) * LANES)
        kx = kx_ref[:, cols]
        vxt = _transpose_bf16(vx_ref[:, cols])
        for g in range(NA_BLOCK_ROWS // NA_GROUP):
            r0 = i * NA_BLOCK_ROWS + g * NA_GROUP
            band_start = jnp.clip(r0 - NA_ROWS // 2, 0, n_rows - NA_GBAND_ROWS)
            off = pl.multiple_of((band_start - (i - 1) * NA_BLOCK_ROWS) * GRID_W, 2 * GRID_W)
            qts, biases = [], []
            for jr in range(NA_GROUP):
                r = r0 + jr
                r_start = jnp.clip(r - NA_ROWS // 2, 0, n_rows - NA_ROWS)
                shift = r - r_start
                lead = pl.multiple_of(NA_PAD - (r_start - band_start) * GRID_W, GRID_W)
                qrows = slice((g * NA_GROUP + jr) * GRID_W, (g * NA_GROUP + jr + 1) * GRID_W)
                qts.append(_pair_queries(q_ref[qrows, cols], low).astype(F32).T)
                biases.append(bias_ref[p, shift, pl.ds(lead, NA_GBAND), :])
            qt = jnp.concatenate(qts, axis=1).astype(BF16)
            s_nb = _dot(kband[pl.ds(off, NA_GBAND), cols], qt) + jnp.concatenate(biases, axis=1)
            s_cx = _dot(kx, qt)
            m = jnp.maximum(jnp.max(s_nb, axis=0, keepdims=True), jnp.max(s_cx, axis=0, keepdims=True))
            p_nb = jnp.exp2(s_nb - m)
            p_cx = jnp.exp2(s_cx - m)
            l = jnp.sum(p_nb, axis=0, keepdims=True) + jnp.sum(p_cx, axis=0, keepdims=True)
            vbt = _transpose_bf16(vband[pl.ds(off, NA_GBAND), cols])
            o = (_dot(vbt, p_nb.astype(BF16)) + _dot(vxt, p_cx.astype(BF16))) / l
            for jr in range(NA_GROUP):
                qrows = slice((g * NA_GROUP + jr) * GRID_W, (g * NA_GROUP + jr + 1) * GRID_W)
                ot = o[:, jr * LANES:(jr + 1) * LANES].T
                o_ref[qrows, cols] = jnp.where(low, ot[:GRID_W], ot[GRID_W:]).astype(BF16)


def _na_bias_table(rpb):
    depth = rpb.shape[0]
    col = jnp.arange(GRID_W)
    c_start = jnp.clip(col - NA_COLS // 2, 0, GRID_W - NA_COLS)
    in_win = (col[None, :] >= c_start[:, None]) & (col[None, :] < c_start[:, None] + NA_COLS)
    dc = col[None, :] - col[:, None] + NA_COLS - 1
    sel_c = ((dc[:, :, None] == jnp.arange(2 * NA_COLS - 1)) & in_win[:, :, None]).astype(F32)
    dr = jnp.arange(NA_ROWS)[None, :] - jnp.arange(NA_ROWS)[:, None] + NA_ROWS - 1
    sel_r = (dr[:, :, None] == jnp.arange(2 * NA_ROWS - 1)).astype(F32)
    pairs = rpb.astype(F32).reshape(depth, NA_HEADS // 2, 2, 2 * NA_ROWS - 1, 2 * NA_COLS - 1) * LOG2_E
    t = jnp.einsum("sir,lpurd,ckd->lpsikuc", sel_r, pairs, sel_c, precision=lax.Precision.HIGHEST)
    t = jnp.where(in_win.T[:, None, :], t, MASK_VALUE).reshape(depth, NA_HEADS // 2, NA_ROWS, NA_BAND, LANES)
    return jnp.pad(t, ((0, 0), (0, 0), (0, 0), (NA_PAD, NA_PAD), (0, 0)), constant_values=MASK_VALUE)


def _na(na, na_c, bias, batch, seq_len, ctx_len):
    n = na.shape[0]
    n_rows = seq_len // GRID_W
    blocks = seq_len // NA_BLOCK
    tok = lambda col: (lambda b, i: (b * blocks + i, col))
    prev = lambda col: (lambda b, i: (b * blocks + jnp.maximum(i - 1, 0), col))
    nxt = lambda col: (lambda b, i: (b * blocks + jnp.minimum(i + 1, blocks - 1), col))
    blk = lambda f: pl.BlockSpec((NA_BLOCK, NA_WIDTH), f)
    return pl.pallas_call(
        functools.partial(_na_kernel, n_rows=n_rows),
        out_shape=jax.ShapeDtypeStruct((n, NA_WIDTH), BF16),
        grid=(batch, blocks),
        in_specs=[blk(tok(0)),
                  blk(prev(1)), blk(tok(1)), blk(nxt(1)),
                  blk(prev(2)), blk(tok(2)), blk(nxt(2)),
                  pl.BlockSpec((ctx_len, NA_WIDTH), lambda b, i: (b, 1)),
                  pl.BlockSpec((ctx_len, NA_WIDTH), lambda b, i: (b, 2)),
                  _const_spec(bias.shape)],
        out_specs=blk(tok(0)),
        scratch_shapes=[pltpu.VMEM((3 * NA_BLOCK, NA_WIDTH), BF16),
                        pltpu.VMEM((3 * NA_BLOCK, NA_WIDTH), BF16)],
        compiler_params=_params(2),
        name="natten",
    )(na, na, na, na, na, na, na, na_c, na_c, bias)


def _outproj_kernel(pp_ref, pm_ref, pn_ref, yna_ref, ygq_ref, x_ref, g1_ref, wout_ref, pw_ref, ps_ref,
                    lng_ref, lnb_ref, o_ref, *, seq_len, tm):
    tiles = seq_len // tm
    ti = pl.program_id(0) % tiles
    prev = jnp.where(ti == 0, 0.0, pp_ref[...])
    nxt = jnp.where(ti == tiles - 1, 0.0, pn_ref[...])
    u = jnp.concatenate([prev, pm_ref[...], nxt], axis=0)
    n = tm + 2 * POOL_HALO
    s2 = u + pltpu.roll(u, 1, 0)
    s4 = pltpu.roll(s2, 1, 0) + pltpu.roll(s2, n - 1, 0)
    s8 = pltpu.roll(s4, 2, 0) + pltpu.roll(s4, n - 2, 0)
    s16 = pltpu.roll(s8, 4, 0) + pltpu.roll(s8, n - 4, 0)
    mid = slice(POOL_HALO, POOL_HALO + tm)
    pos = ti * tm + lax.broadcasted_iota(jnp.int32, (tm, POOL_WIDTH), 0)
    grp = jnp.right_shift(lax.broadcasted_iota(jnp.int32, (tm, POOL_WIDTH), 1), 6)
    half = jnp.left_shift(1, grp)
    cnt = jnp.minimum(pos + half - 1, seq_len - 1) - jnp.maximum(pos - half, 0) + 1
    wsum = jnp.where(grp == 0, s2[mid], jnp.where(grp == 1, s4[mid], jnp.where(grp == 2, s8[mid], s16[mid])))
    pooled = wsum / cnt.astype(F32) - u[mid]
    y_pool = _dot(pooled.astype(BF16), pw_ref[...]) * ps_ref[...]
    mix = (_dot(y_pool.astype(BF16), wout_ref[0:POOL_WIDTH])
           + _dot(yna_ref[...], wout_ref[POOL_WIDTH:POOL_WIDTH + NA_WIDTH])
           + _dot(ygq_ref[...], wout_ref[POOL_WIDTH + NA_WIDTH:D_MODEL]))
    y = DEEPNORM_ALPHA * x_ref[...] + g1_ref[0] * mix
    o_ref[...] = _layer_norm(y, lng_ref[...], lnb_ref[...])


def _outproj(pool_in, y_na, y_gqa, x2d, mod, mod_row, w_out, pool_w_bd, pool_scale, ln_g, ln_b, seq_len, tm):
    n = x2d.shape[0]
    tiles = seq_len // tm
    hb = tm // POOL_HALO
    last_hb = n // POOL_HALO - 1
    tok = lambda i: (i, 0)
    return pl.pallas_call(
        functools.partial(_outproj_kernel, seq_len=seq_len, tm=tm),
        out_shape=jax.ShapeDtypeStruct((n, D_MODEL), F32),
        grid=(n // tm,),
        in_specs=[pl.BlockSpec((POOL_HALO, POOL_WIDTH), lambda i: (jnp.maximum(i * hb - 1, 0), 0)),
                  pl.BlockSpec((tm, POOL_WIDTH), tok),
                  pl.BlockSpec((POOL_HALO, POOL_WIDTH), lambda i: (jnp.minimum((i + 1) * hb, last_hb), 0)),
                  pl.BlockSpec((tm, NA_WIDTH), tok),
                  pl.BlockSpec((tm, GQA_WIDTH), tok),
                  pl.BlockSpec((tm, D_MODEL), tok),
                  pl.BlockSpec((1, 1, D_MODEL), lambda i: (mod_row(i // tiles), 0, 2)),
                  _const_spec((D_MODEL, D_MODEL)),
                  _const_spec((POOL_WIDTH, POOL_WIDTH)),
                  _const_spec((1, POOL_WIDTH)),
                  _const_spec((1, D_MODEL)),
                  _const_spec((1, D_MODEL))],
        out_specs=pl.BlockSpec((tm, D_MODEL), tok),
        compiler_params=_params(1),
        name="outproj",
    )(pool_in, pool_in, pool_in, y_na, y_gqa, x2d, mod, w_out, pool_w_bd, pool_scale, ln_g, ln_b)


FFN_HALO = 8


def _ffn_kernel(xp_ref, xm_ref, xn_ref, sh_ref, sc_ref, g2_ref, wup_ref, cw_ref, cb_ref, wdn_ref,
                lng_ref, lnb_ref, o_ref, *, seq_len, tm):
    tiles = seq_len // tm
    ti = pl.program_id(0) % tiles
    scale = 1.0 + sc_ref[0]
    shift = sh_ref[0]
    xm = xm_ref[...]
    hp = jnp.where(ti == 0, 0.0, xp_ref[...] * scale + shift)
    hn = jnp.where(ti == tiles - 1, 0.0, xn_ref[...] * scale + shift)
    h = jnp.concatenate([hp, xm * scale + shift, hn], axis=0).astype(BF16)
    n = tm + 2 * FFN_HALO
    mid = slice(FFN_HALO, FFN_HALO + tm)

    def conv(u, c0):
        cw = cw_ref[:, c0:c0 + FF_CHUNK]
        y = (pltpu.roll(u, 1, 0) * cw[0:1] + cb_ref[:, c0:c0 + FF_CHUNK]
             + u * cw[1:2] + pltpu.roll(u, n - 1, 0) * cw[2:3])
        return y[mid]

    acc = jnp.zeros((tm, D_MODEL), F32)
    for ch in range(D_FF // FF_CHUNK):
        a0 = ch * FF_CHUNK
        g0 = D_FF + a0
        a = conv(_dot(h, wup_ref[:, a0:a0 + FF_CHUNK]), a0)
        g = conv(_dot(h, wup_ref[:, g0:g0 + FF_CHUNK]), g0)
        act = (a * (g * jax.nn.sigmoid(g))).astype(BF16)
        acc = acc + _dot(act, wdn_ref[a0:a0 + FF_CHUNK])
    y = DEEPNORM_ALPHA * xm + g2_ref[0] * acc
    o_ref[...] = _layer_norm(y, lng_ref[...], lnb_ref[...])


def _ffn(x2d, mod, mod_row, w_up, conv_w, conv_b, w_down, ln_g, ln_b, seq_len, tm):
    n = x2d.shape[0]
    tiles = seq_len // tm
    hb = tm // FFN_HALO
    last_hb = n // FFN_HALO - 1
    tok = lambda i: (i, 0)
    mod_spec = lambda k: pl.BlockSpec((1, 1, D_MODEL), lambda i: (mod_row(i // tiles), 0, k))
    return pl.pallas_call(
        functools.partial(_ffn_kernel, seq_len=seq_len, tm=tm),
        out_shape=jax.ShapeDtypeStruct((n, D_MODEL), F32),
        grid=(n // tm,),
        in_specs=[pl.BlockSpec((FFN_HALO, D_MODEL), lambda i: (jnp.maximum(i * hb - 1, 0), 0)),
                  pl.BlockSpec((tm, D_MODEL), tok),
                  pl.BlockSpec((FFN_HALO, D_MODEL), lambda i: (jnp.minimum((i + 1) * hb, last_hb), 0)),
                  mod_spec(3), mod_spec(4), mod_spec(5),
                  _const_spec((D_MODEL, 2 * D_FF)),
                  _const_spec((3, 2 * D_FF)),
                  _const_spec((1, 2 * D_FF)),
                  _const_spec((D_FF, D_MODEL)),
                  _const_spec((1, D_MODEL)),
                  _const_spec((1, D_MODEL))],
        out_specs=pl.BlockSpec((tm, D_MODEL), tok),
        compiler_params=_params(1),
        name="convffn",
    )(x2d, x2d, x2d, mod, mod, mod, w_up, conv_w, conv_b, w_down, ln_g, ln_b)


def _rope_tables(seq_len):
    t = jnp.arange(seq_len, dtype=jnp.int32)
    half = HEAD_DIM // 2
    inv = ROPE_THETA ** (-jnp.arange(0, half, 2, dtype=F32) / half)
    ang_r = (t // GRID_W).astype(F32)[:, None] * inv
    ang_c = (t % GRID_W).astype(F32)[:, None] * inv
    cos = jnp.concatenate([jnp.cos(ang_r)] * 2 + [jnp.cos(ang_c)] * 2, axis=-1)
    sin = jnp.concatenate([-jnp.sin(ang_r), jnp.sin(ang_r), -jnp.sin(ang_c), jnp.sin(ang_c)], axis=-1)
    return jnp.tile(cos, (1, 2)), jnp.tile(sin, (1, 2))


def _pair_head_order():
    heads = jnp.array([0, 3, 1, 4, 2, 5])
    return (heads[:, None] * HEAD_DIM + jnp.arange(HEAD_DIM)[None, :]).reshape(-1)


def kernel(x, c, ctx, c_ctx, w_mod, b_mod, w_in, pool_w, pool_scale, na_rpb, q_norm, k_norm, w_out,
           ln1_g, ln1_b, w_up, conv_w, conv_b, w_down, ln2_g, ln2_b):
    batch, seq_len, d = x.shape
    ctx_len = ctx.shape[1]
    depth = w_mod.shape[0]
    assert d == D_MODEL and seq_len % NA_BLOCK == 0 and seq_len // GRID_W >= NA_GBAND_ROWS
    tm = 512
    tm_c = ctx_len
    tq = 256
    tk = next(t for t in (2816, 768, 256) if (seq_len + ctx_len) % t == 0)
    assert (seq_len + ctx_len) % tk == 0 and seq_len % tq == 0 and ctx_len % 256 == 0

    mod_rows = -(-(batch + 1) // 8) * 8
    c_rows = jnp.zeros((mod_rows, d), F32).at[:batch].set(c).at[batch].set(c_ctx)
    mods = _modulation(c_rows, w_mod, b_mod)
    lat_row = lambda b: b
    ctx_row = lambda b: batch

    order = _pair_head_order()
    q_cols = _C_GQ + order
    w_in_p = jnp.concatenate([w_in[:, :, :_C_GQ], w_in[:, :, q_cols], w_in[:, :, _C_GK:]], axis=-1).astype(BF16)
    w_out_p = w_out.astype(BF16)
    w_up_b = w_up.astype(BF16)
    w_down_b = w_down.astype(BF16)
    eye = jnp.eye(pool_w.shape[1], dtype=F32)
    pool_w_bd = jnp.einsum("gh,lgcd->lgchd", eye, pool_w).reshape(depth, POOL_WIDTH, POOL_WIDTH).astype(BF16)

    na_bias = _na_bias_table(na_rpb)
    cos, sin = _rope_tables(seq_len)
    cos_c = jnp.ones((ctx_len, LANES), F32)
    sin_c = jnp.zeros((ctx_len, LANES), F32)

    xl = x.reshape(batch * seq_len, d)
    xc = ctx.reshape(batch * ctx_len, d)
    for l in range(depth):
        mod = mods[l].reshape(mod_rows, 1, 6 * d)
        qg = jnp.tile(q_norm[l], 2).reshape(1, LANES)
        kg = jnp.tile(k_norm[l], 2).reshape(1, LANES)
        ps = pool_scale[l].reshape(1, POOL_WIDTH)
        g1, b1 = ln1_g[l].reshape(1, d), ln1_b[l].reshape(1, d)
        g2, b2 = ln2_g[l].reshape(1, d), ln2_b[l].reshape(1, d)
        cb = conv_b[l].reshape(1, 2 * D_FF)

        p_c, na_c, q_c, k_c, v_c = _inproj(xc, mod, ctx_row, w_in_p[l], cos_c, sin_c, qg, kg, ctx_len, tm_c)
        p_l, na_l, q_l, k_l, v_l = _inproj(xl, mod, lat_row, w_in_p[l], cos, sin, qg, kg, seq_len, tm)

        k_c3 = k_c.reshape(batch, ctx_len, GQA_KV_WIDTH)
        v_c3 = v_c.reshape(batch, ctx_len, GQA_KV_WIDTH)
        k_all = jnp.concatenate([k_c3, k_l.reshape(batch, seq_len, GQA_KV_WIDTH)], axis=1)
        v_all = jnp.concatenate([v_c3, v_l.reshape(batch, seq_len, GQA_KV_WIDTH)], axis=1)
        y_gqa = _gqa(q_l, k_all, v_all, q_norm[l], k_norm[l], seq_len, tq, tk)
        y_na = _na(na_l, na_c, na_bias[l], batch, seq_len, ctx_len)
        x1 = _outproj(p_l, y_na, y_gqa, xl, mod, lat_row, w_out_p[l], pool_w_bd[l], ps, g1, b1, seq_len, tm)
        xl = _ffn(x1, mod, lat_row, w_up_b[l], conv_w[l], cb, w_down_b[l], g2, b2, seq_len, tm)

        if l < depth - 1:
            y_na_c = _ctx_mha(na_c, ctx_len)
            y_gqa_c = _gqa(q_c, k_c3, v_c3, q_norm[l], k_norm[l], ctx_len, ctx_len, ctx_len)
            xc1 = _outproj(p_c, y_na_c, y_gqa_c, xc, mod, ctx_row, w_out_p[l], pool_w_bd[l], ps, g1, b1,
                           ctx_len, tm_c)
            xc = _ffn(xc1, mod, ctx_row, w_up_b[l], conv_w[l], cb, w_down_b[l], g2, b2, ctx_len, tm_c)
    return xl.reshape(batch, seq_len, d)
```

```python
import functools

import jax
import jax.numpy as jnp
from jax import lax
from jax.experimental import pallas as pl
from jax.experimental.pallas import tpu as pltpu

F32 = jnp.float32
BF16 = jnp.bfloat16

D_MODEL = 1024
GRID_W = 64
HEAD_DIM = 64
LANES = 128
POOL_WIDTH = 256
POOL_WINDOWS = (2, 4, 8, 16)
POOL_HALO = 8
NA_WIDTH = 384
NA_HEADS = 6
NA_ROWS = 8
NA_COLS = 16
GQA_WIDTH = 384
GQA_KV_WIDTH = 128
D_FF = 2816
FF_CHUNK = 2816
ROPE_THETA = 10000.0
DEEPNORM_ALPHA = 8.0 ** 0.25
LN_EPS = 1e-6
MASK_VALUE = -1e30
LOG2_E = 1.4426950408889634
VMEM_LIMIT = 56 * 1024 * 1024

_C_POOL = 0
_C_NA = POOL_WIDTH
_C_GQ = _C_NA + 3 * NA_WIDTH
_C_GK = _C_GQ + GQA_WIDTH
_C_GV = _C_GK + GQA_KV_WIDTH
IN_WIDTH = _C_GV + GQA_KV_WIDTH


def _params(n_axes):
    return pltpu.CompilerParams(dimension_semantics=("parallel",) * n_axes,
                                vmem_limit_bytes=VMEM_LIMIT)


def _const_spec(shape):
    return pl.BlockSpec(shape, lambda *_: (0,) * len(shape), pipeline_mode=pl.Buffered(1))


def _dot(a, b):
    return jnp.dot(a, b, preferred_element_type=F32)


def _dot_nt(a, b):
    return lax.dot_general(a, b, (((1,), (1,)), ((), ())), preferred_element_type=F32)


def _layer_norm(y, g, b):
    mu = jnp.mean(y, axis=-1, keepdims=True)
    d = y - mu
    var = jnp.mean(d * d, axis=-1, keepdims=True)
    return d * lax.rsqrt(var + LN_EPS) * g + b


def _mod_kernel(c_ref, w_ref, b_ref, o_ref):
    c = c_ref[...]
    act = c * jax.nn.sigmoid(c)
    o_ref[0] = _dot(act, w_ref[0]) + b_ref[0]


def _modulation(c_rows, w_mod, b_mod):
    depth, d, n = w_mod.shape
    rows = c_rows.shape[0]
    bn = 1536
    return pl.pallas_call(
        _mod_kernel,
        out_shape=jax.ShapeDtypeStruct((depth, rows, n), F32),
        grid=(depth, n // bn),
        in_specs=[pl.BlockSpec((rows, d), lambda l, j: (0, 0)),
                  pl.BlockSpec((1, d, bn), lambda l, j: (l, 0, j)),
                  pl.BlockSpec((1, 1, bn), lambda l, j: (l, 0, j))],
        out_specs=pl.BlockSpec((1, rows, bn), lambda l, j: (l, 0, j)),
        compiler_params=_params(2),
        name="modulation",
    )(c_rows, w_mod, b_mod.reshape(depth, 1, n))


def _swap16(v):
    lane = lax.broadcasted_iota(jnp.int32, v.shape, 1)
    even = jnp.bitwise_and(lane, 16) == 0
    return jnp.where(even, pltpu.roll(v, LANES - 16, 1), pltpu.roll(v, 16, 1))


def _head_sumsq(z):
    width = z.shape[1]
    r = jnp.right_shift(lax.broadcasted_iota(jnp.int32, (width, width), 0), 6)
    c = jnp.right_shift(lax.broadcasted_iota(jnp.int32, (width, width), 1), 6)
    return _dot((z * z).astype(BF16), (r == c).astype(BF16))


def _norm_rope(z, ss, gain, cos, sin):
    zn = z * lax.rsqrt(ss * (1.0 / HEAD_DIM) + LN_EPS) * gain
    return zn * cos + _swap16(zn) * sin


def _inproj_kernel(x_ref, sh_ref, sc_ref, w_ref, cos_ref, sin_ref, qg_ref, kg_ref,
                   pool_ref, na_ref, vt_ref, q_ref, k_ref, v_ref):
    h = (x_ref[...] * (1.0 + sc_ref[0]) + sh_ref[0]).astype(BF16)
    pool_ref[...] = _dot(h, w_ref[:, _C_POOL:_C_NA])
    na_ref[:, 0:NA_WIDTH] = (_dot(h, w_ref[:, _C_NA:_C_NA + NA_WIDTH])
                             * (HEAD_DIM ** -0.5 * LOG2_E)).astype(BF16)
    na_ref[:, NA_WIDTH:2 * NA_WIDTH] = _dot(h, w_ref[:, _C_NA + NA_WIDTH:_C_NA + 2 * NA_WIDTH]).astype(BF16)
    vt_ref[...] = _dot(h, w_ref[:, _C_NA + 2 * NA_WIDTH:_C_GQ]).T.astype(BF16)
    cos = cos_ref[...]
    sin = sin_ref[...]
    zqk = _dot(h, w_ref[:, _C_GQ:_C_GV])
    ss = jnp.concatenate([_head_sumsq(zqk[:, 0:2 * LANES]), _head_sumsq(zqk[:, 2 * LANES:4 * LANES])], axis=1)
    for j in range(GQA_WIDTH // LANES):
        cols = slice(j * LANES, (j + 1) * LANES)
        blk = _norm_rope(zqk[:, cols], ss[:, cols], qg_ref[...], cos, sin)
        q_ref[:, cols] = (blk * (HEAD_DIM ** -0.5 * LOG2_E)).astype(BF16)
    cols = slice(GQA_WIDTH, GQA_WIDTH + GQA_KV_WIDTH)
    k_ref[...] = _norm_rope(zqk[:, cols], ss[:, cols], kg_ref[...], cos, sin).astype(BF16)
    v_ref[...] = _dot(h, w_ref[:, _C_GV:IN_WIDTH]).astype(BF16)


def _inproj(x2d, mod, mod_row, w_in, cos, sin, qg, kg, seq_len, tm):
    n = x2d.shape[0]
    tiles = seq_len // tm
    tok = lambda i: (i, 0)
    outs = (jax.ShapeDtypeStruct((n, POOL_WIDTH), F32),
            jax.ShapeDtypeStruct((n, 2 * NA_WIDTH), BF16),
            jax.ShapeDtypeStruct((NA_WIDTH, n), BF16),
            jax.ShapeDtypeStruct((n, GQA_WIDTH), BF16),
            jax.ShapeDtypeStruct((n, GQA_KV_WIDTH), BF16),
            jax.ShapeDtypeStruct((n, GQA_KV_WIDTH), BF16))
    return pl.pallas_call(
        _inproj_kernel,
        out_shape=outs,
        grid=(n // tm,),
        in_specs=[pl.BlockSpec((tm, D_MODEL), tok),
                  pl.BlockSpec((1, 1, D_MODEL), lambda i: (mod_row(i // tiles), 0, 0)),
                  pl.BlockSpec((1, 1, D_MODEL), lambda i: (mod_row(i // tiles), 0, 1)),
                  _const_spec((D_MODEL, IN_WIDTH)),
                  pl.BlockSpec((tm, LANES), lambda i: (i % tiles, 0)),
                  pl.BlockSpec((tm, LANES), lambda i: (i % tiles, 0)),
                  _const_spec((1, LANES)),
                  _const_spec((1, LANES))],
        out_specs=[pl.BlockSpec((tm, POOL_WIDTH), tok),
                   pl.BlockSpec((tm, 2 * NA_WIDTH), tok),
                   pl.BlockSpec((NA_WIDTH, tm), lambda i: (0, i)),
                   pl.BlockSpec((tm, GQA_WIDTH), tok),
                   pl.BlockSpec((tm, GQA_KV_WIDTH), tok),
                   pl.BlockSpec((tm, GQA_KV_WIDTH), tok)],
        compiler_params=_params(1),
        name="inproj",
    )(x2d, mod, mod, w_in, cos, sin, qg, kg)


def _gqa_kernel(q_ref, k_ref, vt_ref, o_ref, *, tq, bounded):
    n_k = k_ref.shape[1]
    row = lax.broadcasted_iota(jnp.int32, (LANES, tq), 0)
    low = row < HEAD_DIM
    blocks = [q_ref[:, j * LANES:(j + 1) * LANES].astype(F32).T for j in range(3)]
    qt = jnp.concatenate([jnp.where(low, b, 0.0) for b in blocks]
                         + [jnp.where(low, 0.0, b) for b in blocks], axis=1).astype(BF16)
    heads = 2 * 3

    def pv(i, pb):
        outs = []
        for h in range(heads):
            g = h // 3
            vt = vt_ref[0, i, g * HEAD_DIM:(g + 1) * HEAD_DIM, :]
            outs.append(_dot(vt, pb[:, h * tq:(h + 1) * tq]))
        return outs

    def bounded_body(i, carry):
        lsum, acc = carry
        p = jnp.exp2(_dot(k_ref[0, i], qt))
        lsum = lsum + jnp.sum(p, axis=0, keepdims=True)
        return lsum, acc + jnp.concatenate(pv(i, p.astype(BF16)), axis=0)

    def online_body(i, carry):
        m, l, acc = carry
        s = _dot(k_ref[0, i], qt)
        m_new = jnp.maximum(m, jnp.max(s, axis=0, keepdims=True))
        p = jnp.exp2(s - m_new)
        a = jnp.exp2(m - m_new)
        l = a * l + jnp.sum(p, axis=0, keepdims=True)
        outs = pv(i, p.astype(BF16))
        acc = jnp.concatenate([acc[h * HEAD_DIM:(h + 1) * HEAD_DIM] * a[:, h * tq:(h + 1) * tq] + outs[h]
                               for h in range(heads)], axis=0)
        return m_new, l, acc

    acc0 = jnp.zeros((heads * HEAD_DIM, tq), F32)
    l0 = jnp.zeros((1, heads * tq), F32)
    if bounded:
        l, acc = l0, acc0
        for i in range(n_k):
            l, acc = bounded_body(i, (l, acc))
    else:
        m0 = jnp.full((1, heads * tq), MASK_VALUE, F32)
        _, l, acc = lax.fori_loop(0, n_k, online_body, (m0, l0, acc0))
    out = jnp.concatenate([acc[h * HEAD_DIM:(h + 1) * HEAD_DIM] / l[:, h * tq:(h + 1) * tq]
                           for h in range(heads)], axis=0)
    o_ref[...] = out.T.astype(BF16)


SCORE_BOUND = 60.0


def _gqa(q2d, k3d, v3d, gain_q, gain_k, seq_len, tq, tk):
    batch, lk, _ = k3d.shape
    k4d = k3d.reshape(batch, lk // tk, tk, GQA_KV_WIDTH)
    vt4d = v3d.reshape(batch, lk // tk, tk, GQA_KV_WIDTH).transpose(0, 1, 3, 2)
    bound = HEAD_DIM ** 0.5 * jnp.max(jnp.abs(gain_q)) * jnp.max(jnp.abs(gain_k))
    return lax.cond(bound <= SCORE_BOUND,
                    lambda: _gqa_call(q2d, k4d, vt4d, seq_len, tq, True),
                    lambda: _gqa_call(q2d, k4d, vt4d, seq_len, tq, False))


def _gqa_call(q2d, k4d, vt4d, seq_len, tq, bounded):
    n = q2d.shape[0]
    _, n_k, tk, _ = k4d.shape
    tiles = seq_len // tq
    return pl.pallas_call(
        functools.partial(_gqa_kernel, tq=tq, bounded=bounded),
        out_shape=jax.ShapeDtypeStruct((n, GQA_WIDTH), BF16),
        grid=(n // tq,),
        in_specs=[pl.BlockSpec((tq, GQA_WIDTH), lambda i: (i, 0)),
                  pl.BlockSpec((1, n_k, tk, GQA_KV_WIDTH), lambda i: (i // tiles, 0, 0, 0)),
                  pl.BlockSpec((1, n_k, GQA_KV_WIDTH, tk), lambda i: (i // tiles, 0, 0, 0))],
        out_specs=pl.BlockSpec((tq, GQA_WIDTH), lambda i: (i, 0)),
        compiler_params=_params(1),
        name="gqa",
    )(q2d, k4d, vt4d)


def _pair_queries(qb, low):
    zero = jnp.zeros_like(qb)
    return jnp.concatenate([jnp.where(low, qb, zero), jnp.where(low, zero, qb)], axis=0)


def _transpose_bf16(a):
    return a.astype(F32).T.astype(BF16)


def _ctx_mha_kernel(q_ref, k_ref, vt_ref, o_ref):
    rows = q_ref.shape[0]
    lane = lax.broadcasted_iota(jnp.int32, (rows, LANES), 1)
    low = lane < HEAD_DIM
    for j in range(NA_WIDTH // LANES):
        cols = slice(j * LANES, (j + 1) * LANES)
        qt = _transpose_bf16(_pair_queries(q_ref[:, cols], low))
        s = _dot(k_ref[:, cols], qt)
        p = jnp.exp2(s - jnp.max(s, axis=0, keepdims=True))
        l = jnp.sum(p, axis=0, keepdims=True)
        ot = (_dot(vt_ref[cols, :], p.astype(BF16)) / l).T
        o_ref[:, cols] = jnp.where(low, ot[:rows], ot[rows:]).astype(BF16)


def _ctx_mha(na_c, vt_c, ctx_len):
    n = na_c.shape[0]
    return pl.pallas_call(
        _ctx_mha_kernel,
        out_shape=jax.ShapeDtypeStruct((n, NA_WIDTH), BF16),
        grid=(n // ctx_len,),
        in_specs=[pl.BlockSpec((ctx_len, NA_WIDTH), lambda b: (b, 0)),
                  pl.BlockSpec((ctx_len, NA_WIDTH), lambda b: (b, 1)),
                  pl.BlockSpec((NA_WIDTH, ctx_len), lambda b: (0, b))],
        out_specs=pl.BlockSpec((ctx_len, NA_WIDTH), lambda b: (b, 0)),
        compiler_params=_params(1),
        name="ctx_mha",
    )(na_c, na_c, vt_c)


NA_BLOCK_ROWS = 8
NA_BLOCK = NA_BLOCK_ROWS * GRID_W
NA_BAND = NA_ROWS * GRID_W
NA_GROUP = 4
NA_GBAND_ROWS = NA_ROWS + NA_GROUP
NA_GBAND = NA_GBAND_ROWS * GRID_W
NA_PAD = NA_GBAND - NA_BAND
NA_VBLK = 2 * GRID_W


def _na_kernel(q_ref, kp_ref, kc_ref, kn_ref, vp_ref, vc_ref, vn_ref, kx_ref, vx_ref, bias_ref,
               o_ref, kband, vband, *, n_rows):
    i = pl.program_id(1)
    kband[0:NA_BLOCK] = kp_ref[...]
    kband[NA_BLOCK:2 * NA_BLOCK] = kc_ref[...]
    kband[2 * NA_BLOCK:3 * NA_BLOCK] = kn_ref[...]
    per_blk = NA_BLOCK // NA_VBLK
    for j, ref in enumerate((vp_ref, vc_ref, vn_ref)):
        for t in range(per_blk):
            vband[j * per_blk + t] = ref[:, t * NA_VBLK:(t + 1) * NA_VBLK]
    lane = lax.broadcasted_iota(jnp.int32, (GRID_W, LANES), 1)
    low = lane < HEAD_DIM

    for p in range(NA_WIDTH // LANES):
        cols = slice(p * LANES, (p + 1) * LANES)
        for g in range(NA_BLOCK_ROWS // NA_GROUP):
            r0 = i * NA_BLOCK_ROWS + g * NA_GROUP
            band_start = jnp.clip(r0 - NA_ROWS // 2, 0, n_rows - NA_GBAND_ROWS)
            off = pl.multiple_of((band_start - (i - 1) * NA_BLOCK_ROWS) * GRID_W, NA_GROUP * GRID_W)
            vblk = (band_start - (i - 1) * NA_BLOCK_ROWS) // 2
            qts, biases = [], []
            for jr in range(NA_GROUP):
                r = r0 + jr
                r_start = jnp.clip(r - NA_ROWS // 2, 0, n_rows - NA_ROWS)
                shift = r - r_start
                lead = pl.multiple_of(NA_PAD - (r_start - band_start) * GRID_W, GRID_W)
                qrows = slice((g * NA_GROUP + jr) * GRID_W, (g * NA_GROUP + jr + 1) * GRID_W)
                qts.append(_pair_queries(q_ref[qrows, cols], low).astype(F32).T)
                biases.append(bias_ref[p, shift, pl.ds(lead, NA_GBAND), :])
            qt = jnp.concatenate(qts, axis=1).astype(BF16)
            keys = jnp.concatenate([kband[pl.ds(off, NA_GBAND), cols], kx_ref[:, cols]], axis=0)
            s = _dot(keys, qt)
            s = jnp.concatenate([s[:NA_GBAND] + jnp.concatenate(biases, axis=1), s[NA_GBAND:]], axis=0)
            pt = jnp.exp2(s - jnp.max(s, axis=0, keepdims=True))
            l = jnp.sum(pt, axis=0, keepdims=True)
            vt = jnp.concatenate([vband[vblk + t, cols, :] for t in range(NA_GBAND // NA_VBLK)]
                                 + [vx_ref[cols, :]], axis=1)
            o = _dot(vt, pt.astype(BF16)) / l
            for jr in range(NA_GROUP):
                qrows = slice((g * NA_GROUP + jr) * GRID_W, (g * NA_GROUP + jr + 1) * GRID_W)
                ot = o[:, jr * LANES:(jr + 1) * LANES].T
                o_ref[qrows, cols] = jnp.where(low, ot[:GRID_W], ot[GRID_W:]).astype(BF16)


def _na_bias_table(rpb):
    depth = rpb.shape[0]
    col = jnp.arange(GRID_W)
    c_start = jnp.clip(col - NA_COLS // 2, 0, GRID_W - NA_COLS)
    in_win = (col[None, :] >= c_start[:, None]) & (col[None, :] < c_start[:, None] + NA_COLS)
    dc = col[None, :] - col[:, None] + NA_COLS - 1
    sel_c = ((dc[:, :, None] == jnp.arange(2 * NA_COLS - 1)) & in_win[:, :, None]).astype(F32)
    dr = jnp.arange(NA_ROWS)[None, :] - jnp.arange(NA_ROWS)[:, None] + NA_ROWS - 1
    sel_r = (dr[:, :, None] == jnp.arange(2 * NA_ROWS - 1)).astype(F32)
    pairs = rpb.astype(F32).reshape(depth, NA_HEADS // 2, 2, 2 * NA_ROWS - 1, 2 * NA_COLS - 1) * LOG2_E
    t = jnp.einsum("sir,lpurd,ckd->lpsikuc", sel_r, pairs, sel_c, precision=lax.Precision.HIGHEST)
    t = jnp.where(in_win.T[:, None, :], t, MASK_VALUE).reshape(depth, NA_HEADS // 2, NA_ROWS, NA_BAND, LANES)
    return jnp.pad(t, ((0, 0), (0, 0), (0, 0), (NA_PAD, NA_PAD), (0, 0)), constant_values=MASK_VALUE)


def _na(na, vt, na_c, vt_c, bias, batch, seq_len, ctx_len):
    n = na.shape[0]
    n_rows = seq_len // GRID_W
    blocks = seq_len // NA_BLOCK
    cur = lambda b, i: b * blocks + i
    prev = lambda b, i: b * blocks + jnp.maximum(i - 1, 0)
    nxt = lambda b, i: b * blocks + jnp.minimum(i + 1, blocks - 1)
    rows = lambda f, col: pl.BlockSpec((NA_BLOCK, NA_WIDTH), lambda b, i: (f(b, i), col))
    lanes = lambda f: pl.BlockSpec((NA_WIDTH, NA_BLOCK), lambda b, i: (0, f(b, i)))
    return pl.pallas_call(
        functools.partial(_na_kernel, n_rows=n_rows),
        out_shape=jax.ShapeDtypeStruct((n, NA_WIDTH), BF16),
        grid=(batch, blocks),
        in_specs=[rows(cur, 0),
                  rows(prev, 1), rows(cur, 1), rows(nxt, 1),
                  lanes(prev), lanes(cur), lanes(nxt),
                  pl.BlockSpec((ctx_len, NA_WIDTH), lambda b, i: (b, 1)),
                  pl.BlockSpec((NA_WIDTH, ctx_len), lambda b, i: (0, b)),
                  _const_spec(bias.shape)],
        out_specs=rows(cur, 0),
        scratch_shapes=[pltpu.VMEM((3 * NA_BLOCK, NA_WIDTH), BF16),
                        pltpu.VMEM((3 * NA_BLOCK // NA_VBLK, NA_WIDTH, NA_VBLK), BF16)],
        compiler_params=_params(2),
        name="natten",
    )(na, na, na, na, vt, vt, vt, na_c, vt_c, bias)


def _outproj_kernel(pp_ref, pm_ref, pn_ref, yna_ref, ygq_ref, x_ref, g1_ref, wout_ref, pw_ref, ps_ref,
                    lng_ref, lnb_ref, o_ref, *, seq_len, tm):
    tiles = seq_len // tm
    ti = pl.program_id(0) % tiles
    prev = jnp.where(ti == 0, 0.0, pp_ref[...])
    nxt = jnp.where(ti == tiles - 1, 0.0, pn_ref[...])
    u = jnp.concatenate([prev, pm_ref[...], nxt], axis=0)
    n = tm + 2 * POOL_HALO
    s2 = u + pltpu.roll(u, 1, 0)
    s4 = pltpu.roll(s2, 1, 0) + pltpu.roll(s2, n - 1, 0)
    s8 = pltpu.roll(s4, 2, 0) + pltpu.roll(s4, n - 2, 0)
    s16 = pltpu.roll(s8, 4, 0) + pltpu.roll(s8, n - 4, 0)
    mid = slice(POOL_HALO, POOL_HALO + tm)
    pos = ti * tm + lax.broadcasted_iota(jnp.int32, (tm, POOL_WIDTH), 0)
    grp = jnp.right_shift(lax.broadcasted_iota(jnp.int32, (tm, POOL_WIDTH), 1), 6)
    half = jnp.left_shift(1, grp)
    cnt = jnp.minimum(pos + half - 1, seq_len - 1) - jnp.maximum(pos - half, 0) + 1
    wsum = jnp.where(grp == 0, s2[mid], jnp.where(grp == 1, s4[mid], jnp.where(grp == 2, s8[mid], s16[mid])))
    pooled = wsum / cnt.astype(F32) - u[mid]
    y_pool = _dot(pooled.astype(BF16), pw_ref[...]) * ps_ref[...]
    mix = (_dot(y_pool.astype(BF16), wout_ref[0:POOL_WIDTH])
           + _dot(yna_ref[...], wout_ref[POOL_WIDTH:POOL_WIDTH + NA_WIDTH])
           + _dot(ygq_ref[...], wout_ref[POOL_WIDTH + NA_WIDTH:D_MODEL]))
    y = DEEPNORM_ALPHA * x_ref[...] + g1_ref[0] * mix
    o_ref[...] = _layer_norm(y, lng_ref[...], lnb_ref[...])


def _outproj(pool_in, y_na, y_gqa, x2d, mod, mod_row, w_out, pool_w_bd, pool_scale, ln_g, ln_b, seq_len, tm):
    n = x2d.shape[0]
    tiles = seq_len // tm
    hb = tm // POOL_HALO
    last_hb = n // POOL_HALO - 1
    tok = lambda i: (i, 0)
    return pl.pallas_call(
        functools.partial(_outproj_kernel, seq_len=seq_len, tm=tm),
        out_shape=jax.ShapeDtypeStruct((n, D_MODEL), F32),
        grid=(n // tm,),
        in_specs=[pl.BlockSpec((POOL_HALO, POOL_WIDTH), lambda i: (jnp.maximum(i * hb - 1, 0), 0)),
                  pl.BlockSpec((tm, POOL_WIDTH), tok),
                  pl.BlockSpec((POOL_HALO, POOL_WIDTH), lambda i: (jnp.minimum((i + 1) * hb, last_hb), 0)),
                  pl.BlockSpec((tm, NA_WIDTH), tok),
                  pl.BlockSpec((tm, GQA_WIDTH), tok),
                  pl.BlockSpec((tm, D_MODEL), tok),
                  pl.BlockSpec((1, 1, D_MODEL), lambda i: (mod_row(i // tiles), 0, 2)),
                  _const_spec((D_MODEL, D_MODEL)),
                  _const_spec((POOL_WIDTH, POOL_WIDTH)),
                  _const_spec((1, POOL_WIDTH)),
                  _const_spec((1, D_MODEL)),
                  _const_spec((1, D_MODEL))],
        out_specs=pl.BlockSpec((tm, D_MODEL), tok),
        compiler_params=_params(1),
        name="outproj",
    )(pool_in, pool_in, pool_in, y_na, y_gqa, x2d, mod, w_out, pool_w_bd, pool_scale, ln_g, ln_b)


FFN_HALO = 8


def _ffn_kernel(xp_ref, xm_ref, xn_ref, sh_ref, sc_ref, g2_ref, wup_ref, cw_ref, cb_ref, wdn_ref,
                lng_ref, lnb_ref, o_ref, *, seq_len, tm):
    tiles = seq_len // tm
    ti = pl.program_id(0) % tiles
    scale = 1.0 + sc_ref[0]
    shift = sh_ref[0]
    xm = xm_ref[...]
    hp = jnp.where(ti == 0, 0.0, xp_ref[...] * scale + shift)
    hn = jnp.where(ti == tiles - 1, 0.0, xn_ref[...] * scale + shift)
    h = jnp.concatenate([hp, xm * scale + shift, hn], axis=0).astype(BF16)
    n = tm + 2 * FFN_HALO
    mid = slice(FFN_HALO, FFN_HALO + tm)

    def conv(u, c0):
        cw = cw_ref[:, c0:c0 + FF_CHUNK]
        y = (pltpu.roll(u, 1, 0) * cw[0:1] + cb_ref[:, c0:c0 + FF_CHUNK]
             + u * cw[1:2] + pltpu.roll(u, n - 1, 0) * cw[2:3])
        return y[mid]

    acc = jnp.zeros((tm, D_MODEL), F32)
    for ch in range(D_FF // FF_CHUNK):
        a0 = ch * FF_CHUNK
        g0 = D_FF + a0
        a = conv(_dot(h, wup_ref[:, a0:a0 + FF_CHUNK]), a0)
        g = conv(_dot(h, wup_ref[:, g0:g0 + FF_CHUNK]), g0)
        act = (a * (g * jax.nn.sigmoid(g))).astype(BF16)
        acc = acc + _dot(act, wdn_ref[a0:a0 + FF_CHUNK])
    y = DEEPNORM_ALPHA * xm + g2_ref[0] * acc
    o_ref[...] = _layer_norm(y, lng_ref[...], lnb_ref[...])


def _ffn(x2d, mod, mod_row, w_up, conv_w, conv_b, w_down, ln_g, ln_b, seq_len, tm):
    n = x2d.shape[0]
    tiles = seq_len // tm
    hb = tm // FFN_HALO
    last_hb = n // FFN_HALO - 1
    tok = lambda i: (i, 0)
    mod_spec = lambda k: pl.BlockSpec((1, 1, D_MODEL), lambda i: (mod_row(i // tiles), 0, k))
    return pl.pallas_call(
        functools.partial(_ffn_kernel, seq_len=seq_len, tm=tm),
        out_shape=jax.ShapeDtypeStruct((n, D_MODEL), F32),
        grid=(n // tm,),
        in_specs=[pl.BlockSpec((FFN_HALO, D_MODEL), lambda i: (jnp.maximum(i * hb - 1, 0), 0)),
                  pl.BlockSpec((tm, D_MODEL), tok),
                  pl.BlockSpec((FFN_HALO, D_MODEL), lambda i: (jnp.minimum((i + 1) * hb, last_hb), 0)),
                  mod_spec(3), mod_spec(4), mod_spec(5),
                  _const_spec((D_MODEL, 2 * D_FF)),
                  _const_spec((3, 2 * D_FF)),
                  _const_spec((1, 2 * D_FF)),
                  _const_spec((D_FF, D_MODEL)),
                  _const_spec((1, D_MODEL)),
                  _const_spec((1, D_MODEL))],
        out_specs=pl.BlockSpec((tm, D_MODEL), tok),
        compiler_params=_params(1),
        name="convffn",
    )(x2d, x2d, x2d, mod, mod, mod, w_up, conv_w, conv_b, w_down, ln_g, ln_b)


def _rope_tables(seq_len):
    t = jnp.arange(seq_len, dtype=jnp.int32)
    half = HEAD_DIM // 2
    inv = ROPE_THETA ** (-jnp.arange(0, half, 2, dtype=F32) / half)
    ang_r = (t // GRID_W).astype(F32)[:, None] * inv
    ang_c = (t % GRID_W).astype(F32)[:, None] * inv
    cos = jnp.concatenate([jnp.cos(ang_r)] * 2 + [jnp.cos(ang_c)] * 2, axis=-1)
    sin = jnp.concatenate([-jnp.sin(ang_r), jnp.sin(ang_r), -jnp.sin(ang_c), jnp.sin(ang_c)], axis=-1)
    return jnp.tile(cos, (1, 2)), jnp.tile(sin, (1, 2))


def _pair_head_order():
    heads = jnp.array([0, 3, 1, 4, 2, 5])
    return (heads[:, None] * HEAD_DIM + jnp.arange(HEAD_DIM)[None, :]).reshape(-1)


def kernel(x, c, ctx, c_ctx, w_mod, b_mod, w_in, pool_w, pool_scale, na_rpb, q_norm, k_norm, w_out,
           ln1_g, ln1_b, w_up, conv_w, conv_b, w_down, ln2_g, ln2_b):
    batch, seq_len, d = x.shape
    ctx_len = ctx.shape[1]
    depth = w_mod.shape[0]
    assert d == D_MODEL and seq_len % NA_BLOCK == 0 and seq_len // GRID_W >= NA_GBAND_ROWS
    tm = 512
    tm_c = ctx_len
    tq = 256
    tk = next(t for t in (2816, 768, 256) if (seq_len + ctx_len) % t == 0)
    assert (seq_len + ctx_len) % tk == 0 and seq_len % tq == 0 and ctx_len % 256 == 0

    mod_rows = -(-(batch + 1) // 8) * 8
    c_rows = jnp.zeros((mod_rows, d), F32).at[:batch].set(c).at[batch].set(c_ctx)
    mods = _modulation(c_rows, w_mod, b_mod)
    lat_row = lambda b: b
    ctx_row = lambda b: batch

    order = _pair_head_order()
    q_cols = _C_GQ + order
    w_in_p = jnp.concatenate([w_in[:, :, :_C_GQ], w_in[:, :, q_cols], w_in[:, :, _C_GK:]], axis=-1).astype(BF16)
    w_out_p = w_out.astype(BF16)
    w_up_b = w_up.astype(BF16)
    w_down_b = w_down.astype(BF16)
    eye = jnp.eye(pool_w.shape[1], dtype=F32)
    pool_w_bd = jnp.einsum("gh,lgcd->lgchd", eye, pool_w).reshape(depth, POOL_WIDTH, POOL_WIDTH).astype(BF16)

    na_bias = _na_bias_table(na_rpb)
    cos, sin = _rope_tables(seq_len)
    cos_c = jnp.ones((ctx_len, LANES), F32)
    sin_c = jnp.zeros((ctx_len, LANES), F32)

    xl = x.reshape(batch * seq_len, d)
    xc = ctx.reshape(batch * ctx_len, d)
    for l in range(depth):
        mod = mods[l].reshape(mod_rows, 1, 6 * d)
        qg = jnp.tile(q_norm[l], 2).reshape(1, LANES)
        kg = jnp.tile(k_norm[l], 2).reshape(1, LANES)
        ps = pool_scale[l].reshape(1, POOL_WIDTH)
        g1, b1 = ln1_g[l].reshape(1, d), ln1_b[l].reshape(1, d)
        g2, b2 = ln2_g[l].reshape(1, d), ln2_b[l].reshape(1, d)
        cb = conv_b[l].reshape(1, 2 * D_FF)

        p_c, na_c, vt_c, q_c, k_c, v_c = _inproj(xc, mod, ctx_row, w_in_p[l], cos_c, sin_c, qg, kg, ctx_len, tm_c)
        p_l, na_l, vt_l, q_l, k_l, v_l = _inproj(xl, mod, lat_row, w_in_p[l], cos, sin, qg, kg, seq_len, tm)

        k_c3 = k_c.reshape(batch, ctx_len, GQA_KV_WIDTH)
        v_c3 = v_c.reshape(batch, ctx_len, GQA_KV_WIDTH)
        k_all = jnp.concatenate([k_c3, k_l.reshape(batch, seq_len, GQA_KV_WIDTH)], axis=1)
        v_all = jnp.concatenate([v_c3, v_l.reshape(batch, seq_len, GQA_KV_WIDTH)], axis=1)
        y_gqa = _gqa(q_l, k_all, v_all, q_norm[l], k_norm[l], seq_len, tq, tk)
        y_na = _na(na_l, vt_l, na_c, vt_c, na_bias[l], batch, seq_len, ctx_len)
        x1 = _outproj(p_l, y_na, y_gqa, xl, mod, lat_row, w_out_p[l], pool_w_bd[l], ps, g1, b1, seq_len, tm)
        xl = _ffn(x1, mod, lat_row, w_up_b[l], conv_w[l], cb, w_down_b[l], g2, b2, seq_len, tm)

        if l < depth - 1:
            y_na_c = _ctx_mha(na_c, vt_c, ctx_len)
            y_gqa_c = _gqa(q_c, k_c3, v_c3, q_norm[l], k_norm[l], ctx_len, ctx_len, ctx_len)
            xc1 = _outproj(p_c, y_na_c, y_gqa_c, xc, mod, ctx_row, w_out_p[l], pool_w_bd[l], ps, g1, b1,
                           ctx_len, tm_c)
            xc = _ffn(xc1, mod, ctx_row, w_up_b[l], conv_w[l], cb, w_down_b[l], g2, b2, ctx_len, tm_c)
    return xl.reshape(batch, seq_len, d)
```

```python
import functools

import jax
import jax.numpy as jnp
from jax import lax
from jax.experimental import pallas as pl
from jax.experimental.pallas import tpu as pltpu

F32 = jnp.float32
BF16 = jnp.bfloat16

D_MODEL = 1024
GRID_W = 64
HEAD_DIM = 64
LANES = 128
POOL_WIDTH = 256
POOL_WINDOWS = (2, 4, 8, 16)
POOL_HALO = 8
NA_WIDTH = 384
NA_HEADS = 6
NA_ROWS = 8
NA_COLS = 16
GQA_WIDTH = 384
GQA_KV_WIDTH = 128
D_FF = 2816
FF_CHUNK = 256
ROPE_THETA = 10000.0
DEEPNORM_ALPHA = 8.0 ** 0.25
LN_EPS = 1e-6
MASK_VALUE = -1e30
LOG2_E = 1.4426950408889634
VMEM_LIMIT = 56 * 1024 * 1024

_C_POOL = 0
_C_NA = POOL_WIDTH
_C_GQ = _C_NA + 3 * NA_WIDTH
_C_GK = _C_GQ + GQA_WIDTH
_C_GV = _C_GK + GQA_KV_WIDTH
IN_WIDTH = _C_GV + GQA_KV_WIDTH


def _params(n_axes):
    return pltpu.CompilerParams(dimension_semantics=("parallel",) * n_axes,
                                vmem_limit_bytes=VMEM_LIMIT)


def _const_spec(shape):
    return pl.BlockSpec(shape, lambda *_: (0,) * len(shape), pipeline_mode=pl.Buffered(1))


def _dot(a, b):
    return jnp.dot(a, b, preferred_element_type=F32)


def _dot_nt(a, b):
    return lax.dot_general(a, b, (((1,), (1,)), ((), ())), preferred_element_type=F32)


def _layer_norm(y, g, b):
    mu = jnp.mean(y, axis=-1, keepdims=True)
    d = y - mu
    var = jnp.mean(d * d, axis=-1, keepdims=True)
    return d * lax.rsqrt(var + LN_EPS) * g + b


def _mod_kernel(c_ref, w_ref, b_ref, o_ref):
    c = c_ref[...]
    act = c * jax.nn.sigmoid(c)
    o_ref[0] = _dot(act, w_ref[0]) + b_ref[0]


def _modulation(c_rows, w_mod, b_mod):
    depth, d, n = w_mod.shape
    rows = c_rows.shape[0]
    bn = 1536
    return pl.pallas_call(
        _mod_kernel,
        out_shape=jax.ShapeDtypeStruct((depth, rows, n), F32),
        grid=(depth, n // bn),
        in_specs=[pl.BlockSpec((rows, d), lambda l, j: (0, 0)),
                  pl.BlockSpec((1, d, bn), lambda l, j: (l, 0, j)),
                  pl.BlockSpec((1, 1, bn), lambda l, j: (l, 0, j))],
        out_specs=pl.BlockSpec((1, rows, bn), lambda l, j: (l, 0, j)),
        compiler_params=_params(2),
        name="modulation",
    )(c_rows, w_mod, b_mod.reshape(depth, 1, n))


def _swap16(v):
    lane = lax.broadcasted_iota(jnp.int32, v.shape, 1)
    even = jnp.bitwise_and(lane, 16) == 0
    return jnp.where(even, pltpu.roll(v, LANES - 16, 1), pltpu.roll(v, 16, 1))


def _head_sumsq(z):
    width = z.shape[1]
    r = jnp.right_shift(lax.broadcasted_iota(jnp.int32, (width, width), 0), 6)
    c = jnp.right_shift(lax.broadcasted_iota(jnp.int32, (width, width), 1), 6)
    return _dot((z * z).astype(BF16), (r == c).astype(BF16))


def _norm_rope(z, ss, gain, cos, sin):
    zn = z * lax.rsqrt(ss * (1.0 / HEAD_DIM) + LN_EPS) * gain
    return zn * cos + _swap16(zn) * sin


def _inproj_kernel(x_ref, sh_ref, sc_ref, w_ref, cos_ref, sin_ref, qg_ref, kg_ref,
                   pool_ref, na_ref, vt_ref, q_ref, k_ref, v_ref):
    h = (x_ref[...] * (1.0 + sc_ref[0]) + sh_ref[0]).astype(BF16)
    zqk = _dot(h, w_ref[:, _C_GQ:_C_GV])
    ss = jnp.concatenate([_head_sumsq(zqk[:, 0:2 * LANES]), _head_sumsq(zqk[:, 2 * LANES:4 * LANES])], axis=1)
    cos = cos_ref[...]
    sin = sin_ref[...]
    for j in range(GQA_WIDTH // LANES):
        cols = slice(j * LANES, (j + 1) * LANES)
        blk = _norm_rope(zqk[:, cols], ss[:, cols], qg_ref[...], cos, sin)
        q_ref[:, cols] = (blk * (HEAD_DIM ** -0.5 * LOG2_E)).astype(BF16)
    cols = slice(GQA_WIDTH, GQA_WIDTH + GQA_KV_WIDTH)
    k_ref[...] = _norm_rope(zqk[:, cols], ss[:, cols], kg_ref[...], cos, sin).astype(BF16)
    vt_ref[...] = _dot(h, w_ref[:, _C_NA + 2 * NA_WIDTH:_C_GQ]).astype(BF16).T
    na_ref[:, 0:NA_WIDTH] = (_dot(h, w_ref[:, _C_NA:_C_NA + NA_WIDTH])
                             * (HEAD_DIM ** -0.5 * LOG2_E)).astype(BF16)
    na_ref[:, NA_WIDTH:2 * NA_WIDTH] = _dot(h, w_ref[:, _C_NA + NA_WIDTH:_C_NA + 2 * NA_WIDTH]).astype(BF16)
    v_ref[...] = _dot(h, w_ref[:, _C_GV:IN_WIDTH]).astype(BF16)
    pool_ref[...] = _dot(h, w_ref[:, _C_POOL:_C_NA])


def _inproj(x2d, mod, mod_row, w_in, cos, sin, qg, kg, seq_len, tm):
    n = x2d.shape[0]
    tiles = seq_len // tm
    tok = lambda i: (i, 0)
    outs = (jax.ShapeDtypeStruct((n, POOL_WIDTH), F32),
            jax.ShapeDtypeStruct((n, 2 * NA_WIDTH), BF16),
            jax.ShapeDtypeStruct((NA_WIDTH, n), BF16),
            jax.ShapeDtypeStruct((n, GQA_WIDTH), BF16),
            jax.ShapeDtypeStruct((n, GQA_KV_WIDTH), BF16),
            jax.ShapeDtypeStruct((n, GQA_KV_WIDTH), BF16))
    return pl.pallas_call(
        _inproj_kernel,
        out_shape=outs,
        grid=(n // tm,),
        in_specs=[pl.BlockSpec((tm, D_MODEL), tok),
                  pl.BlockSpec((1, 1, D_MODEL), lambda i: (mod_row(i // tiles), 0, 0)),
                  pl.BlockSpec((1, 1, D_MODEL), lambda i: (mod_row(i // tiles), 0, 1)),
                  _const_spec((D_MODEL, IN_WIDTH)),
                  pl.BlockSpec((tm, LANES), lambda i: (i % tiles, 0)),
                  pl.BlockSpec((tm, LANES), lambda i: (i % tiles, 0)),
                  _const_spec((1, LANES)),
                  _const_spec((1, LANES))],
        out_specs=[pl.BlockSpec((tm, POOL_WIDTH), tok),
                   pl.BlockSpec((tm, 2 * NA_WIDTH), tok),
                   pl.BlockSpec((NA_WIDTH, tm), lambda i: (0, i)),
                   pl.BlockSpec((tm, GQA_WIDTH), tok),
                   pl.BlockSpec((tm, GQA_KV_WIDTH), tok),
                   pl.BlockSpec((tm, GQA_KV_WIDTH), tok)],
        compiler_params=_params(1),
        name="inproj",
    )(x2d, mod, mod, w_in, cos, sin, qg, kg)


def _gqa_kernel(q_ref, k_ref, vt_ref, o_ref, *, tq, bounded):
    n_k = k_ref.shape[1]
    row = lax.broadcasted_iota(jnp.int32, (LANES, tq), 0)
    low = row < HEAD_DIM
    blocks = [q_ref[:, j * LANES:(j + 1) * LANES].astype(F32).T for j in range(3)]
    qt = jnp.concatenate([jnp.where(low, b, 0.0) for b in blocks]
                         + [jnp.where(low, 0.0, b) for b in blocks], axis=1).astype(BF16)
    heads = 2 * 3

    def pv(i, pb):
        outs = []
        for h in range(heads):
            g = h // 3
            vt = vt_ref[0, i, g * HEAD_DIM:(g + 1) * HEAD_DIM, :]
            outs.append(_dot(vt, pb[:, h * tq:(h + 1) * tq]))
        return outs

    def bounded_body(i, carry):
        lsum, acc = carry
        p = jnp.exp2(_dot(k_ref[0, i], qt))
        lsum = lsum + jnp.sum(p, axis=0, keepdims=True)
        return lsum, acc + jnp.concatenate(pv(i, p.astype(BF16)), axis=0)

    def online_body(i, carry):
        m, l, acc = carry
        s = _dot(k_ref[0, i], qt)
        m_new = jnp.maximum(m, jnp.max(s, axis=0, keepdims=True))
        p = jnp.exp2(s - m_new)
        a = jnp.exp2(m - m_new)
        l = a * l + jnp.sum(p, axis=0, keepdims=True)
        outs = pv(i, p.astype(BF16))
        acc = jnp.concatenate([acc[h * HEAD_DIM:(h + 1) * HEAD_DIM] * a[:, h * tq:(h + 1) * tq] + outs[h]
                               for h in range(heads)], axis=0)
        return m_new, l, acc

    acc0 = jnp.zeros((heads * HEAD_DIM, tq), F32)
    l0 = jnp.zeros((1, heads * tq), F32)
    if bounded:
        l, acc = l0, acc0
        for i in range(n_k):
            l, acc = bounded_body(i, (l, acc))
    else:
        m0 = jnp.full((1, heads * tq), MASK_VALUE, F32)
        _, l, acc = lax.fori_loop(0, n_k, online_body, (m0, l0, acc0))
    out = jnp.concatenate([acc[h * HEAD_DIM:(h + 1) * HEAD_DIM] / l[:, h * tq:(h + 1) * tq]
                           for h in range(heads)], axis=0)
    o_ref[...] = out.T.astype(BF16)


SCORE_BOUND = 60.0


def _gqa(q2d, k3d, v3d, gain_q, gain_k, seq_len, tq, tk):
    batch, lk, _ = k3d.shape
    k4d = k3d.reshape(batch, lk // tk, tk, GQA_KV_WIDTH)
    vt4d = v3d.reshape(batch, lk // tk, tk, GQA_KV_WIDTH).transpose(0, 1, 3, 2)
    bound = HEAD_DIM ** 0.5 * jnp.max(jnp.abs(gain_q)) * jnp.max(jnp.abs(gain_k))
    return lax.cond(bound <= SCORE_BOUND,
                    lambda: _gqa_call(q2d, k4d, vt4d, seq_len, tq, True),
                    lambda: _gqa_call(q2d, k4d, vt4d, seq_len, tq, False))


def _gqa_call(q2d, k4d, vt4d, seq_len, tq, bounded):
    n = q2d.shape[0]
    _, n_k, tk, _ = k4d.shape
    tiles = seq_len // tq
    return pl.pallas_call(
        functools.partial(_gqa_kernel, tq=tq, bounded=bounded),
        out_shape=jax.ShapeDtypeStruct((n, GQA_WIDTH), BF16),
        grid=(n // tq,),
        in_specs=[pl.BlockSpec((tq, GQA_WIDTH), lambda i: (i, 0)),
                  pl.BlockSpec((1, n_k, tk, GQA_KV_WIDTH), lambda i: (i // tiles, 0, 0, 0)),
                  pl.BlockSpec((1, n_k, GQA_KV_WIDTH, tk), lambda i: (i // tiles, 0, 0, 0))],
        out_specs=pl.BlockSpec((tq, GQA_WIDTH), lambda i: (i, 0)),
        compiler_params=_params(1),
        name="gqa",
    )(q2d, k4d, vt4d)


def _pair_queries(qb, low):
    zero = jnp.zeros_like(qb)
    return jnp.concatenate([jnp.where(low, qb, zero), jnp.where(low, zero, qb)], axis=0)


def _transpose_bf16(a):
    return a.astype(F32).T.astype(BF16)


def _ctx_mha_kernel(q_ref, k_ref, vt_ref, o_ref):
    rows = q_ref.shape[0]
    lane = lax.broadcasted_iota(jnp.int32, (rows, LANES), 1)
    low = lane < HEAD_DIM
    for j in range(NA_WIDTH // LANES):
        cols = slice(j * LANES, (j + 1) * LANES)
        qt = _transpose_bf16(_pair_queries(q_ref[:, cols], low))
        s = _dot(k_ref[:, cols], qt)
        p = jnp.exp2(s - jnp.max(s, axis=0, keepdims=True))
        l = jnp.sum(p, axis=0, keepdims=True)
        ot = (_dot(vt_ref[cols, :], p.astype(BF16)) / l).T
        o_ref[:, cols] = jnp.where(low, ot[:rows], ot[rows:]).astype(BF16)


def _ctx_mha(na_c, vt_c, ctx_len):
    n = na_c.shape[0]
    return pl.pallas_call(
        _ctx_mha_kernel,
        out_shape=jax.ShapeDtypeStruct((n, NA_WIDTH), BF16),
        grid=(n // ctx_len,),
        in_specs=[pl.BlockSpec((ctx_len, NA_WIDTH), lambda b: (b, 0)),
                  pl.BlockSpec((ctx_len, NA_WIDTH), lambda b: (b, 1)),
                  pl.BlockSpec((NA_WIDTH, ctx_len), lambda b: (0, b))],
        out_specs=pl.BlockSpec((ctx_len, NA_WIDTH), lambda b: (b, 0)),
        compiler_params=_params(1),
        name="ctx_mha",
    )(na_c, na_c, vt_c)


NA_BLOCK_ROWS = 8
NA_BLOCK = NA_BLOCK_ROWS * GRID_W
NA_BAND = NA_ROWS * GRID_W
NA_GROUP = 4
NA_GBAND_ROWS = NA_ROWS + NA_GROUP
NA_GBAND = NA_GBAND_ROWS * GRID_W
NA_PAD = NA_GBAND - NA_BAND
NA_VBLK = 2 * GRID_W


def _na_kernel(q_ref, kp_ref, kc_ref, kn_ref, vp_ref, vc_ref, vn_ref, kx_ref, vx_ref, bias_ref,
               o_ref, kband, vband, *, n_rows):
    i = pl.program_id(1)
    kband[0:NA_BLOCK] = kp_ref[...]
    kband[NA_BLOCK:2 * NA_BLOCK] = kc_ref[...]
    kband[2 * NA_BLOCK:3 * NA_BLOCK] = kn_ref[...]
    per_blk = NA_BLOCK // NA_VBLK
    for j, ref in enumerate((vp_ref, vc_ref, vn_ref)):
        for t in range(per_blk):
            vband[j * per_blk + t] = ref[:, t * NA_VBLK:(t + 1) * NA_VBLK]
    lane = lax.broadcasted_iota(jnp.int32, (GRID_W, LANES), 1)
    low = lane < HEAD_DIM

    for p in range(NA_WIDTH // LANES):
        cols = slice(p * LANES, (p + 1) * LANES)
        for g in range(NA_BLOCK_ROWS // NA_GROUP):
            r0 = i * NA_BLOCK_ROWS + g * NA_GROUP
            band_start = jnp.clip(r0 - NA_ROWS // 2, 0, n_rows - NA_GBAND_ROWS)
            off = pl.multiple_of((band_start - (i - 1) * NA_BLOCK_ROWS) * GRID_W, NA_GROUP * GRID_W)
            vblk = (band_start - (i - 1) * NA_BLOCK_ROWS) // 2
            qts, biases = [], []
            for jr in range(NA_GROUP):
                r = r0 + jr
                r_start = jnp.clip(r - NA_ROWS // 2, 0, n_rows - NA_ROWS)
                shift = r - r_start
                lead = pl.multiple_of(NA_PAD - (r_start - band_start) * GRID_W, GRID_W)
                qrows = slice((g * NA_GROUP + jr) * GRID_W, (g * NA_GROUP + jr + 1) * GRID_W)
                qts.append(_pair_queries(q_ref[qrows, cols], low).astype(F32).T)
                biases.append(bias_ref[p, shift, pl.ds(lead, NA_GBAND), :])
            qt = jnp.concatenate(qts, axis=1).astype(BF16)
            keys = jnp.concatenate([kband[pl.ds(off, NA_GBAND), cols], kx_ref[:, cols]], axis=0)
            s = _dot(keys, qt)
            s = jnp.concatenate([s[:NA_GBAND] + jnp.concatenate(biases, axis=1), s[NA_GBAND:]], axis=0)
            pt = jnp.exp2(s - jnp.max(s, axis=0, keepdims=True))
            l = jnp.sum(pt, axis=0, keepdims=True)
            vt = jnp.concatenate([vband[vblk + t, cols, :] for t in range(NA_GBAND // NA_VBLK)]
                                 + [vx_ref[cols, :]], axis=1)
            o = _dot(vt, pt.astype(BF16)) / l
            for jr in range(NA_GROUP):
                qrows = slice((g * NA_GROUP + jr) * GRID_W, (g * NA_GROUP + jr + 1) * GRID_W)
                ot = o[:, jr * LANES:(jr + 1) * LANES].T
                o_ref[qrows, cols] = jnp.where(low, ot[:GRID_W], ot[GRID_W:]).astype(BF16)


def _na_bias_table(rpb):
    depth = rpb.shape[0]
    col = jnp.arange(GRID_W)
    c_start = jnp.clip(col - NA_COLS // 2, 0, GRID_W - NA_COLS)
    in_win = (col[None, :] >= c_start[:, None]) & (col[None, :] < c_start[:, None] + NA_COLS)
    dc = col[None, :] - col[:, None] + NA_COLS - 1
    sel_c = ((dc[:, :, None] == jnp.arange(2 * NA_COLS - 1)) & in_win[:, :, None]).astype(F32)
    dr = jnp.arange(NA_ROWS)[None, :] - jnp.arange(NA_ROWS)[:, None] + NA_ROWS - 1
    sel_r = (dr[:, :, None] == jnp.arange(2 * NA_ROWS - 1)).astype(F32)
    pairs = rpb.astype(F32).reshape(depth, NA_HEADS // 2, 2, 2 * NA_ROWS - 1, 2 * NA_COLS - 1) * LOG2_E
    t = jnp.einsum("sir,lpurd,ckd->lpsikuc", sel_r, pairs, sel_c, precision=lax.Precision.HIGHEST)
    t = jnp.where(in_win.T[:, None, :], t, MASK_VALUE).reshape(depth, NA_HEADS // 2, NA_ROWS, NA_BAND, LANES)
    return jnp.pad(t, ((0, 0), (0, 0), (0, 0), (NA_PAD, NA_PAD), (0, 0)), constant_values=MASK_VALUE)


def _na(na, vt, na_c, vt_c, bias, batch, seq_len, ctx_len):
    n = na.shape[0]
    n_rows = seq_len // GRID_W
    blocks = seq_len // NA_BLOCK
    cur = lambda b, i: b * blocks + i
    prev = lambda b, i: b * blocks + jnp.maximum(i - 1, 0)
    nxt = lambda b, i: b * blocks + jnp.minimum(i + 1, blocks - 1)
    rows = lambda f, col: pl.BlockSpec((NA_BLOCK, NA_WIDTH), lambda b, i: (f(b, i), col))
    lanes = lambda f: pl.BlockSpec((NA_WIDTH, NA_BLOCK), lambda b, i: (0, f(b, i)))
    return pl.pallas_call(
        functools.partial(_na_kernel, n_rows=n_rows),
        out_shape=jax.ShapeDtypeStruct((n, NA_WIDTH), BF16),
        grid=(batch, blocks),
        in_specs=[rows(cur, 0),
                  rows(prev, 1), rows(cur, 1), rows(nxt, 1),
                  lanes(prev), lanes(cur), lanes(nxt),
                  pl.BlockSpec((ctx_len, NA_WIDTH), lambda b, i: (b, 1)),
                  pl.BlockSpec((NA_WIDTH, ctx_len), lambda b, i: (0, b)),
                  _const_spec(bias.shape)],
        out_specs=rows(cur, 0),
        scratch_shapes=[pltpu.VMEM((3 * NA_BLOCK, NA_WIDTH), BF16),
                        pltpu.VMEM((3 * NA_BLOCK // NA_VBLK, NA_WIDTH, NA_VBLK), BF16)],
        compiler_params=_params(2),
        name="natten",
    )(na, na, na, na, vt, vt, vt, na_c, vt_c, bias)


def _outproj_kernel(pp_ref, pm_ref, pn_ref, yna_ref, ygq_ref, x_ref, g1_ref, wout_ref, pw_ref, ps_ref,
                    lng_ref, lnb_ref, o_ref, *, seq_len, tm):
    tiles = seq_len // tm
    ti = pl.program_id(0) % tiles
    prev = jnp.where(ti == 0, 0.0, pp_ref[...])
    nxt = jnp.where(ti == tiles - 1, 0.0, pn_ref[...])
    u = jnp.concatenate([prev, pm_ref[...], nxt], axis=0)
    n = tm + 2 * POOL_HALO
    s2 = u + pltpu.roll(u, 1, 0)
    s4 = pltpu.roll(s2, 1, 0) + pltpu.roll(s2, n - 1, 0)
    s8 = pltpu.roll(s4, 2, 0) + pltpu.roll(s4, n - 2, 0)
    s16 = pltpu.roll(s8, 4, 0) + pltpu.roll(s8, n - 4, 0)
    mid = slice(POOL_HALO, POOL_HALO + tm)
    pos = ti * tm + lax.broadcasted_iota(jnp.int32, (tm, POOL_WIDTH), 0)
    grp = jnp.right_shift(lax.broadcasted_iota(jnp.int32, (tm, POOL_WIDTH), 1), 6)
    half = jnp.left_shift(1, grp)
    cnt = jnp.minimum(pos + half - 1, seq_len - 1) - jnp.maximum(pos - half, 0) + 1
    wsum = jnp.where(grp == 0, s2[mid], jnp.where(grp == 1, s4[mid], jnp.where(grp == 2, s8[mid], s16[mid])))
    pooled = wsum / cnt.astype(F32) - u[mid]
    y_pool = _dot(pooled.astype(BF16), pw_ref[...]) * ps_ref[...]
    mix = (_dot(y_pool.astype(BF16), wout_ref[0:POOL_WIDTH])
           + _dot(yna_ref[...], wout_ref[POOL_WIDTH:POOL_WIDTH + NA_WIDTH])
           + _dot(ygq_ref[...], wout_ref[POOL_WIDTH + NA_WIDTH:D_MODEL]))
    y = DEEPNORM_ALPHA * x_ref[...] + g1_ref[0] * mix
    o_ref[...] = _layer_norm(y, lng_ref[...], lnb_ref[...])


def _outproj(pool_in, y_na, y_gqa, x2d, mod, mod_row, w_out, pool_w_bd, pool_scale, ln_g, ln_b, seq_len, tm):
    n = x2d.shape[0]
    tiles = seq_len // tm
    hb = tm // POOL_HALO
    last_hb = n // POOL_HALO - 1
    tok = lambda i: (i, 0)
    return pl.pallas_call(
        functools.partial(_outproj_kernel, seq_len=seq_len, tm=tm),
        out_shape=jax.ShapeDtypeStruct((n, D_MODEL), F32),
        grid=(n // tm,),
        in_specs=[pl.BlockSpec((POOL_HALO, POOL_WIDTH), lambda i: (jnp.maximum(i * hb - 1, 0), 0)),
                  pl.BlockSpec((tm, POOL_WIDTH), tok),
                  pl.BlockSpec((POOL_HALO, POOL_WIDTH), lambda i: (jnp.minimum((i + 1) * hb, last_hb), 0)),
                  pl.BlockSpec((tm, NA_WIDTH), tok),
                  pl.BlockSpec((tm, GQA_WIDTH), tok),
                  pl.BlockSpec((tm, D_MODEL), tok),
                  pl.BlockSpec((1, 1, D_MODEL), lambda i: (mod_row(i // tiles), 0, 2)),
                  _const_spec((D_MODEL, D_MODEL)),
                  _const_spec((POOL_WIDTH, POOL_WIDTH)),
                  _const_spec((1, POOL_WIDTH)),
                  _const_spec((1, D_MODEL)),
                  _const_spec((1, D_MODEL))],
        out_specs=pl.BlockSpec((tm, D_MODEL), tok),
        compiler_params=_params(1),
        name="outproj",
    )(pool_in, pool_in, pool_in, y_na, y_gqa, x2d, mod, w_out, pool_w_bd, pool_scale, ln_g, ln_b)


FFN_HALO = 8


def _ffn_kernel(xp_ref, xm_ref, xn_ref, sh_ref, sc_ref, g2_ref, wup_ref, cw_ref, cb_ref, wdn_ref,
                lng_ref, lnb_ref, o_ref, *, seq_len, tm):
    tiles = seq_len // tm
    ti = pl.program_id(0) % tiles
    scale = 1.0 + sc_ref[0]
    shift = sh_ref[0]
    xm = xm_ref[...]
    hp = jnp.where(ti == 0, 0.0, xp_ref[...] * scale + shift)
    hn = jnp.where(ti == tiles - 1, 0.0, xn_ref[...] * scale + shift)
    h = jnp.concatenate([hp, xm * scale + shift, hn], axis=0).astype(BF16)
    n = tm + 2 * FFN_HALO
    mid = slice(FFN_HALO, FFN_HALO + tm)

    def conv(u, c0):
        cw = cw_ref[:, c0:c0 + FF_CHUNK]
        y = (pltpu.roll(u, 1, 0) * cw[0:1] + cb_ref[:, c0:c0 + FF_CHUNK]
             + u * cw[1:2] + pltpu.roll(u, n - 1, 0) * cw[2:3])
        return y[mid]

    acts = []
    for ch in range(D_FF // FF_CHUNK):
        a0 = ch * FF_CHUNK
        g0 = D_FF + a0
        a = conv(_dot(h, wup_ref[:, a0:a0 + FF_CHUNK]), a0)
        g = conv(_dot(h, wup_ref[:, g0:g0 + FF_CHUNK]), g0)
        acts.append((a * (g * jax.nn.sigmoid(g))).astype(BF16))
    acc = _dot(jnp.concatenate(acts, axis=1), wdn_ref[...])
    y = DEEPNORM_ALPHA * xm + g2_ref[0] * acc
    o_ref[...] = _layer_norm(y, lng_ref[...], lnb_ref[...])


def _ffn(x2d, mod, mod_row, w_up, conv_w, conv_b, w_down, ln_g, ln_b, seq_len, tm):
    n = x2d.shape[0]
    tiles = seq_len // tm
    hb = tm // FFN_HALO
    last_hb = n // FFN_HALO - 1
    tok = lambda i: (i, 0)
    mod_spec = lambda k: pl.BlockSpec((1, 1, D_MODEL), lambda i: (mod_row(i // tiles), 0, k))
    return pl.pallas_call(
        functools.partial(_ffn_kernel, seq_len=seq_len, tm=tm),
        out_shape=jax.ShapeDtypeStruct((n, D_MODEL), F32),
        grid=(n // tm,),
        in_specs=[pl.BlockSpec((FFN_HALO, D_MODEL), lambda i: (jnp.maximum(i * hb - 1, 0), 0)),
                  pl.BlockSpec((tm, D_MODEL), tok),
                  pl.BlockSpec((FFN_HALO, D_MODEL), lambda i: (jnp.minimum((i + 1) * hb, last_hb), 0)),
                  mod_spec(3), mod_spec(4), mod_spec(5),
                  _const_spec((D_MODEL, 2 * D_FF)),
                  _const_spec((3, 2 * D_FF)),
                  _const_spec((1, 2 * D_FF)),
                  _const_spec((D_FF, D_MODEL)),
                  _const_spec((1, D_MODEL)),
                  _const_spec((1, D_MODEL))],
        out_specs=pl.BlockSpec((tm, D_MODEL), tok),
        compiler_params=_params(1),
        name="convffn",
    )(x2d, x2d, x2d, mod, mod, mod, w_up, conv_w, conv_b, w_down, ln_g, ln_b)


def _rope_tables(seq_len):
    t = jnp.arange(seq_len, dtype=jnp.int32)
    half = HEAD_DIM // 2
    inv = ROPE_THETA ** (-jnp.arange(0, half, 2, dtype=F32) / half)
    ang_r = (t // GRID_W).astype(F32)[:, None] * inv
    ang_c = (t % GRID_W).astype(F32)[:, None] * inv
    cos = jnp.concatenate([jnp.cos(ang_r)] * 2 + [jnp.cos(ang_c)] * 2, axis=-1)
    sin = jnp.concatenate([-jnp.sin(ang_r), jnp.sin(ang_r), -jnp.sin(ang_c), jnp.sin(ang_c)], axis=-1)
    return jnp.tile(cos, (1, 2)), jnp.tile(sin, (1, 2))


def _pair_head_order():
    heads = jnp.array([0, 3, 1, 4, 2, 5])
    return (heads[:, None] * HEAD_DIM + jnp.arange(HEAD_DIM)[None, :]).reshape(-1)


def kernel(x, c, ctx, c_ctx, w_mod, b_mod, w_in, pool_w, pool_scale, na_rpb, q_norm, k_norm, w_out,
           ln1_g, ln1_b, w_up, conv_w, conv_b, w_down, ln2_g, ln2_b):
    batch, seq_len, d = x.shape
    ctx_len = ctx.shape[1]
    depth = w_mod.shape[0]
    assert d == D_MODEL and seq_len % NA_BLOCK == 0 and seq_len // GRID_W >= NA_GBAND_ROWS
    tm = 512
    tm_proj = 1024 if seq_len % 1024 == 0 else tm
    tm_c = ctx_len
    tq = 256
    tk = next(t for t in (2816, 768, 256) if (seq_len + ctx_len) % t == 0)
    assert (seq_len + ctx_len) % tk == 0 and seq_len % tq == 0 and ctx_len % 256 == 0

    mod_rows = -(-(batch + 1) // 8) * 8
    c_rows = jnp.zeros((mod_rows, d), F32).at[:batch].set(c).at[batch].set(c_ctx)
    mods = _modulation(c_rows, w_mod, b_mod)
    lat_row = lambda b: b
    ctx_row = lambda b: batch

    order = _pair_head_order()
    q_cols = _C_GQ + order
    w_in_p = jnp.concatenate([w_in[:, :, :_C_GQ], w_in[:, :, q_cols], w_in[:, :, _C_GK:]], axis=-1).astype(BF16)
    w_out_p = w_out.astype(BF16)
    w_up_b = w_up.astype(BF16)
    w_down_b = w_down.astype(BF16)
    eye = jnp.eye(pool_w.shape[1], dtype=F32)
    pool_w_bd = jnp.einsum("gh,lgcd->lgchd", eye, pool_w).reshape(depth, POOL_WIDTH, POOL_WIDTH).astype(BF16)

    na_bias = _na_bias_table(na_rpb)
    cos, sin = _rope_tables(seq_len)
    cos_c = jnp.ones((ctx_len, LANES), F32)
    sin_c = jnp.zeros((ctx_len, LANES), F32)

    xl = x.reshape(batch * seq_len, d)
    xc = ctx.reshape(batch * ctx_len, d)
    for l in range(depth):
        mod = mods[l].reshape(mod_rows, 1, 6 * d)
        qg = jnp.tile(q_norm[l], 2).reshape(1, LANES)
        kg = jnp.tile(k_norm[l], 2).reshape(1, LANES)
        ps = pool_scale[l].reshape(1, POOL_WIDTH)
        g1, b1 = ln1_g[l].reshape(1, d), ln1_b[l].reshape(1, d)
        g2, b2 = ln2_g[l].reshape(1, d), ln2_b[l].reshape(1, d)
        cb = conv_b[l].reshape(1, 2 * D_FF)

        p_c, na_c, vt_c, q_c, k_c, v_c = _inproj(xc, mod, ctx_row, w_in_p[l], cos_c, sin_c, qg, kg, ctx_len, tm_c)
        p_l, na_l, vt_l, q_l, k_l, v_l = _inproj(xl, mod, lat_row, w_in_p[l], cos, sin, qg, kg, seq_len, tm_proj)

        k_c3 = k_c.reshape(batch, ctx_len, GQA_KV_WIDTH)
        v_c3 = v_c.reshape(batch, ctx_len, GQA_KV_WIDTH)
        k_all = jnp.concatenate([k_c3, k_l.reshape(batch, seq_len, GQA_KV_WIDTH)], axis=1)
        v_all = jnp.concatenate([v_c3, v_l.reshape(batch, seq_len, GQA_KV_WIDTH)], axis=1)
        y_gqa = _gqa(q_l, k_all, v_all, q_norm[l], k_norm[l], seq_len, tq, tk)
        y_na = _na(na_l, vt_l, na_c, vt_c, na_bias[l], batch, seq_len, ctx_len)
        x1 = _outproj(p_l, y_na, y_gqa, xl, mod, lat_row, w_out_p[l], pool_w_bd[l], ps, g1, b1, seq_len, tm_proj)
        xl = _ffn(x1, mod, lat_row, w_up_b[l], conv_w[l], cb, w_down_b[l], g2, b2, seq_len, tm)

        if l < depth - 1:
            y_na_c = _ctx_mha(na_c, vt_c, ctx_len)
            y_gqa_c = _gqa(q_c, k_c3, v_c3, q_norm[l], k_norm[l], ctx_len, ctx_len, ctx_len)
            xc1 = _outproj(p_c, y_na_c, y_gqa_c, xc, mod, ctx_row, w_out_p[l], pool_w_bd[l], ps, g1, b1,
                           ctx_len, tm_c)
            xc = _ffn(xc1, mod, ctx_row, w_up_b[l], conv_w[l], cb, w_down_b[l], g2, b2, ctx_len, tm_c)
    return xl.reshape(batch, seq_len, d)
```
